```python
import math
import jax, jax.numpy as jnp
from jax import lax
import numpy as np

D_MODEL = 1024
BATCH = 16
SEQ = 4096
DEPTH = 4

EPS = 1e-6
D_FF = 2816

RET_HEADS = 4
RET_DK = 128
RET_DV = 256
RET_QK = RET_HEADS * RET_DK
RET_V = RET_HEADS * RET_DV
RET_CHUNK = 128
ROPE_BASE = 10000.0

DIL_CONFIGS = ((128, 1), (512, 4), (2048, 16))
DIL_HEADS_PER_GROUP = 4
DIL_HEADS = DIL_HEADS_PER_GROUP * len(DIL_CONFIGS)
DIL_DH = 128
DIL_WIDTH = DIL_HEADS * DIL_DH
DIL_OUT = DIL_HEADS_PER_GROUP * DIL_DH
N_BUCKETS = 32
MAX_DISTANCE = 1024
NEG = -1e30

IN_SIZES = (RET_QK, RET_QK, RET_V, RET_V, DIL_WIDTH, DIL_WIDTH, DIL_WIDTH, D_MODEL, D_MODEL)
IN_WIDTH = sum(IN_SIZES)
IN_SPLITS = tuple(int(s) for s in np.cumsum(IN_SIZES)[:-1])

kernel_name = "hybrid_retention_dilated_macaron_encoder"


def rmsnorm(x, g):
    x32 = x.astype(jnp.float32)
    r = lax.rsqrt(jnp.mean(x32 * x32, axis=-1, keepdims=True) + EPS)
    return (x32 * r * g).astype(x.dtype)


def swiglu(h, w_gate, w_up, w_down):
    return (jax.nn.silu(h @ w_gate) * (h @ w_up)) @ w_down


def rope(t, pos):
    half = t.shape[-1] // 2
    inv = ROPE_BASE ** (-jnp.arange(half, dtype=jnp.float32) / half)
    ang = pos[:, None] * inv[None, :]
    cos, sin = jnp.cos(ang), jnp.sin(ang)
    t1, t2 = t[..., :half], t[..., half:]
    return jnp.concatenate([t1 * cos - t2 * sin, t1 * sin + t2 * cos], axis=-1)


def _retention_dir(q, k, v, log_gamma, strict):
    B, H, S, dk = q.shape
    dv = v.shape[-1]
    C = RET_CHUNK
    N = S // C
    qc = q.reshape(B, H, N, C, dk)
    kc = k.reshape(B, H, N, C, dk)
    vc = v.reshape(B, H, N, C, dv)
    idx = jnp.arange(C, dtype=jnp.float32)
    diff = idx[:, None] - idx[None, :]
    mask = (diff > 0) if strict else (diff >= 0)
    lg = log_gamma[:, None, None]
    dmat = jnp.where(mask[None], jnp.exp(lg * jnp.maximum(diff, 0.0)[None]), 0.0)
    scores = jnp.einsum('bhncd,bhnjd->bhncj', qc, kc) * dmat[None, :, None]
    o_inner = jnp.einsum('bhncj,bhnje->bhnce', scores, vc)
    q_decay = jnp.exp(log_gamma[:, None] * (idx + 1.0)[None])
    k_decay = jnp.exp(log_gamma[:, None] * (C - 1.0 - idx)[None])
    chunk_decay = jnp.exp(log_gamma * C)

    def step(state, xs):
        qn, kn, vn = xs
        cross = jnp.einsum('bhcd,bhde->bhce', qn * q_decay[None, :, :, None], state)
        state = state * chunk_decay[None, :, None, None] + jnp.einsum(
            'bhcd,bhce->bhde', kn * k_decay[None, :, :, None], vn)
        return state, cross

    xs = (jnp.moveaxis(qc, 2, 0), jnp.moveaxis(kc, 2, 0), jnp.moveaxis(vc, 2, 0))
    state0 = jnp.zeros((B, H, dk, dv), jnp.float32)
    _, cross = lax.scan(step, state0, xs)
    out = o_inner + jnp.moveaxis(cross, 0, 2)
    return out.reshape(B, H, S, dv)


def retention(q, k, v, g, decay_exp, norm_g):
    B, S, _ = q.shape
    dtype = q.dtype
    pos = jnp.arange(S, dtype=jnp.float32)
    heads = lambda t, d: t.astype(jnp.float32).reshape(B, S, RET_HEADS, d).transpose(0, 2, 1, 3)
    qh = rope(heads(q, RET_DK), pos)
    kh = rope(heads(k, RET_DK), pos) * (RET_DK ** -0.5)
    vh = heads(v, RET_DV)
    log_gamma = jnp.log1p(-jnp.exp2(-decay_exp.astype(jnp.float32)))
    fwd = _retention_dir(qh, kh, vh, log_gamma[0], strict=False)
    flip = lambda t: jnp.flip(t, axis=2)
    bwd = flip(_retention_dir(flip(qh), flip(kh), flip(vh), log_gamma[1], strict=True))
    o = fwd + bwd
    o = o * lax.rsqrt(jnp.mean(o * o, axis=-1, keepdims=True) + EPS)
    o = o.transpose(0, 2, 1, 3).reshape(B, S, RET_V) * norm_g
    return (jax.nn.silu(g.astype(jnp.float32)) * o).astype(dtype)


def t5_bucket(rel):
    nb = N_BUCKETS // 2
    max_exact = nb // 2
    ret = jnp.where(rel > 0, nb, 0)
    n = jnp.abs(rel)
    nf = jnp.maximum(n, 1).astype(jnp.float32)
    large = max_exact + (jnp.log(nf / max_exact) / math.log(MAX_DISTANCE / max_exact)
                         * (nb - max_exact)).astype(jnp.int32)
    large = jnp.minimum(large, nb - 1)
    return ret + jnp.where(n < max_exact, n, large)


def _dilated_group(q, k, v, bias, d, R):
    B, Hg, S, dh = q.shape
    L = S // d
    nb = -(-L // R)
    Lp = nb * R

    def by_residue(t):
        t = t.reshape(B, Hg, L, d, dh).transpose(0, 1, 3, 2, 4)
        return jnp.pad(t, ((0, 0), (0, 0), (0, 0), (0, Lp - L), (0, 0)))

    def windows(t):
        t = jnp.pad(by_residue(t), ((0, 0), (0, 0), (0, 0), (R, R), (0, 0)))
        t = t.reshape(B, Hg, d, nb + 2, R, dh)
        return jnp.concatenate([t[:, :, :, 0:nb], t[:, :, :, 1:nb + 1], t[:, :, :, 2:nb + 2]], axis=-2)

    qs = by_residue(q).reshape(B, Hg, d, nb, R, dh)
    kw = windows(k)
    vw = windows(v).astype(jnp.float32)
    n_i = jnp.arange(nb)[:, None, None]
    a_i = jnp.arange(R)[None, :, None]
    c_i = jnp.arange(3 * R)[None, None, :]
    j_abs = (n_i - 1) * R + c_i
    valid = (j_abs >= 0) & (j_abs < L) & (jnp.abs(c_i - R - a_i) <= R)
    s = jnp.einsum('bhrnid,bhrnjd->bhrnij', qs, kw).astype(jnp.float32) + bias[None, :, None, None]
    s = jnp.where(valid[None, None, None], s, NEG)
    m = jnp.max(s, axis=-1, keepdims=True)
    p = jnp.exp(s - m)
    den = jnp.sum(p, axis=-1)
    o = jnp.einsum('bhrnij,bhrnjd->bhrnid', p, vw) / den[..., None]
    lse = m[..., 0] + jnp.log(den)
    o = o.reshape(B, Hg, d, Lp, dh)[:, :, :, :L].transpose(0, 1, 3, 2, 4).reshape(B, Hg, S, dh)
    lse = lse.reshape(B, Hg, d, Lp)[..., :L].transpose(0, 1, 3, 2).reshape(B, Hg, S)
    return o, lse


def dilated_attention(q, k, v, q_norm, k_norm, rel_bias):
    B, S, _ = q.shape
    dtype = q.dtype
    heads = lambda t: t.reshape(B, S, DIL_HEADS, DIL_DH).transpose(0, 2, 1, 3)
    qh = rmsnorm(heads(q), q_norm) * (DIL_DH ** -0.5)
    kh = rmsnorm(heads(k), k_norm)
    vh = heads(v)
    outs, lses = [], []
    for gi, (window, d) in enumerate(DIL_CONFIGS):
        R = window // (2 * d)
        sl = slice(gi * DIL_HEADS_PER_GROUP, (gi + 1) * DIL_HEADS_PER_GROUP)
        a_i = jnp.arange(R)[:, None]
        c_i = jnp.arange(3 * R)[None, :]
        bucket = t5_bucket((c_i - R - a_i) * d)
        bias = rel_bias[bucket][..., sl].transpose(2, 0, 1).astype(jnp.float32)
        o, lse = _dilated_group(qh[:, sl], kh[:, sl], vh[:, sl], bias, d, R)
        outs.append(o)
        lses.append(lse)
    w = jax.nn.softmax(jnp.stack(lses, axis=0), axis=0)
    out = jnp.sum(w[..., None] * jnp.stack(outs, axis=0), axis=0)
    return out.transpose(0, 2, 1, 3).reshape(B, S, DIL_OUT).astype(dtype)


def setup_inputs(seed: int = 0) -> dict:
    key = jax.random.key(seed)
    ks = jax.random.split(key, 24)
    f32 = jnp.float32
    nrm = lambda k, shape, fan_in: jax.random.normal(k, shape, f32) * (fan_in ** -0.5)
    gain = lambda k, shape: 1.0 + 0.02 * jax.random.normal(k, shape, f32)
    decay_exp = (5.0 + jnp.arange(RET_HEADS, dtype=f32))[None, None, :] + \
        0.25 * jax.random.uniform(ks[9], (DEPTH, 2, RET_HEADS), f32)
    return {
        "x": jax.random.normal(ks[0], (BATCH, SEQ, D_MODEL), f32),
        "rel_bias": 0.2 * jax.random.normal(ks[1], (N_BUCKETS, DIL_HEADS), f32),
        "norm_ffn1": gain(ks[2], (DEPTH, D_MODEL)),
        "ffn1_gate": nrm(ks[3], (DEPTH, D_MODEL, D_FF), D_MODEL),
        "ffn1_up": nrm(ks[4], (DEPTH, D_MODEL, D_FF), D_MODEL),
        "ffn1_down": nrm(ks[5], (DEPTH, D_FF, D_MODEL), D_FF),
        "norm_mix": gain(ks[6], (DEPTH, D_MODEL)),
        "w_in": nrm(ks[7], (DEPTH, D_MODEL, IN_WIDTH), D_MODEL),
        "b_gate": 0.02 * jax.random.normal(ks[8], (DEPTH, 2 * D_MODEL), f32),
        "ret_decay_exp": decay_exp,
        "ret_norm": gain(ks[10], (DEPTH, RET_V)),
        "w_ret_o": nrm(ks[11], (DEPTH, RET_V, D_MODEL), RET_V),
        "dil_q_norm": gain(ks[12], (DEPTH, DIL_DH)),
        "dil_k_norm": gain(ks[13], (DEPTH, DIL_DH)),
        "w_dil_o": nrm(ks[14], (DEPTH, DIL_OUT, D_MODEL), DIL_OUT),
        "w_out": nrm(ks[15], (DEPTH, D_MODEL, D_MODEL), D_MODEL),
        "norm_ffn2": gain(ks[16], (DEPTH, D_MODEL)),
        "ffn2_gate": nrm(ks[17], (DEPTH, D_MODEL, D_FF), D_MODEL),
        "ffn2_up": nrm(ks[18], (DEPTH, D_MODEL, D_FF), D_MODEL),
        "ffn2_down": nrm(ks[19], (DEPTH, D_FF, D_MODEL), D_FF),
    }


def reference(x, rel_bias, norm_ffn1, ffn1_gate, ffn1_up, ffn1_down, norm_mix, w_in,
              b_gate, ret_decay_exp, ret_norm, w_ret_o, dil_q_norm, dil_k_norm, w_dil_o,
              w_out, norm_ffn2, ffn2_gate, ffn2_up, ffn2_down):
    for l in range(DEPTH):
        x = x + 0.5 * swiglu(rmsnorm(x, norm_ffn1[l]), ffn1_gate[l], ffn1_up[l], ffn1_down[l])
        h = rmsnorm(x, norm_mix[l])
        z = h @ w_in[l]
        rq, rk, rv, rg, dq, dk, dv, g_ret, g_dil = jnp.split(z, IN_SPLITS, axis=-1)
        y_ret = retention(rq, rk, rv, rg, ret_decay_exp[l], ret_norm[l]) @ w_ret_o[l]
        y_dil = dilated_attention(dq, dk, dv, dil_q_norm[l], dil_k_norm[l], rel_bias) @ w_dil_o[l]
        gates = jax.nn.sigmoid(jnp.concatenate([g_ret, g_dil], axis=-1) + b_gate[l])
        merged = gates[..., :D_MODEL] * y_ret + gates[..., D_MODEL:] * y_dil
        x = x + merged @ w_out[l]
        x = x + 0.5 * swiglu(rmsnorm(x, norm_ffn2[l]), ffn2_gate[l], ffn2_up[l], ffn2_down[l])
    return x
```

```python
import functools
import math

import jax
import jax.numpy as jnp
from jax import lax
from jax.experimental import pallas as pl
from jax.experimental.pallas import tpu as pltpu

F32 = jnp.float32
BF16 = jnp.bfloat16

V7X_LANES = 128
V7X_VMEM_BYTES = 64 * 1024 * 1024

EPS = 1e-6
ROPE_BASE = 10000.0
NEG = -1e30

RET_HEADS = 4
RET_DK = 128
RET_DV = 256
RET_CHUNK = 128

DIL_CONFIGS = ((128, 1), (512, 4), (2048, 16))
DIL_GROUP_HEADS = 4
DIL_DH = 128
DIL_R = 64
DIL_QBLK = 2 * DIL_R
DIL_KWIN = 4 * DIL_R
N_BUCKETS = 32
MAX_DISTANCE = 1024

TOKEN_TILE = 512
FFN_CHUNK = 512


def _cparams(vmem_bytes, ngrid):
    return pltpu.CompilerParams(
        dimension_semantics=("arbitrary",) * ngrid,
        vmem_limit_bytes=int(min(vmem_bytes, V7X_VMEM_BYTES - (4 << 20))),
    )


def _dot(a, b):
    return jnp.dot(a, b, preferred_element_type=F32)


def _dot_nt(a, b):
    return lax.dot_general(a, b, (((1,), (1,)), ((), ())), preferred_element_type=F32)


def _dot_tn(a, b):
    return lax.dot_general(a, b, (((0,), (0,)), ((), ())), preferred_element_type=F32)


def _rms(x32, g):
    ms = jnp.mean(x32 * x32, axis=-1, keepdims=True)
    return x32 * lax.rsqrt(ms + EPS) * g


def _sigmoid(x):
    return 1.0 / (1.0 + jnp.exp(-x))


def _norm_kernel(x_ref, g_ref, o_ref):
    o_ref[...] = _rms(x_ref[...], g_ref[...]).astype(o_ref.dtype)


def _rmsnorm(x, g):
    T, D = x.shape
    tm = TOKEN_TILE
    return pl.pallas_call(
        _norm_kernel,
        grid=(T // tm,),
        in_specs=[pl.BlockSpec((tm, D), lambda i: (i, 0)),
                  pl.BlockSpec((1, D), lambda i: (0, 0))],
        out_specs=pl.BlockSpec((tm, D), lambda i: (i, 0)),
        out_shape=jax.ShapeDtypeStruct((T, D), BF16),
        compiler_params=_cparams(16 << 20, 1),
        name="rmsnorm",
    )(x, g.reshape(1, D))


def _ffn_kernel(x_ref, h_ref, wg_ref, wu_ref, wd_ref, gn_ref, xo_ref, ho_ref, *, chunks):
    h = h_ref[...]
    acc = None
    for c0, cw in chunks:
        g = _dot(h, wg_ref[:, c0:c0 + cw])
        u = _dot(h, wu_ref[:, c0:c0 + cw])
        a = (g * _sigmoid(g) * u).astype(BF16)
        d = _dot(a, wd_ref[c0:c0 + cw, :])
        acc = d if acc is None else acc + d
    xn = x_ref[...] + 0.5 * acc
    xo_ref[...] = xn
    ho_ref[...] = _rms(xn, gn_ref[...]).astype(ho_ref.dtype)


def _ffn(x, h, wg, wu, wd, g_next):
    T, D = x.shape
    Fh = wg.shape[1]
    tm = TOKEN_TILE
    chunks = tuple((c0, min(FFN_CHUNK, Fh - c0)) for c0 in range(0, Fh, FFN_CHUNK))
    vmem = 3 * D * Fh * 2 + 2 * tm * D * (4 + 2 + 4 + 2) + 8 * tm * D * 4
    return pl.pallas_call(
        functools.partial(_ffn_kernel, chunks=chunks),
        grid=(T // tm,),
        in_specs=[pl.BlockSpec((tm, D), lambda i: (i, 0)),
                  pl.BlockSpec((tm, D), lambda i: (i, 0)),
                  pl.BlockSpec((D, Fh), lambda i: (0, 0)),
                  pl.BlockSpec((D, Fh), lambda i: (0, 0)),
                  pl.BlockSpec((Fh, D), lambda i: (0, 0)),
                  pl.BlockSpec((1, D), lambda i: (0, 0))],
        out_specs=[pl.BlockSpec((tm, D), lambda i: (i, 0)),
                   pl.BlockSpec((tm, D), lambda i: (i, 0))],
        out_shape=[jax.ShapeDtypeStruct((T, D), F32),
                   jax.ShapeDtypeStruct((T, D), BF16)],
        compiler_params=_cparams(vmem, 1),
        name="ffn",
    )(x, h, wg, wu, wd, g_next.reshape(1, D))


def _proj_retqk_kernel(h_ref, w_ref, cos_ref, sin_ref, o_ref, *, kscale):
    h = h_ref[...]
    cos = cos_ref[...]
    sin = sin_ref[...]
    nh = o_ref.shape[1]
    for half in range(2):
        res = _dot(h, w_ref[:, half * 512:(half + 1) * 512])
        for k in range(4):
            t = res[:, k * RET_DK:(k + 1) * RET_DK]
            r = t * cos + pltpu.roll(t, RET_DK // 2, axis=1) * sin
            if half == 1:
                r = r * kscale
            o_ref[0, half * (nh // 2) + k] = r.astype(o_ref.dtype)


def _proj_retvg_kernel(h_ref, w_ref, o_ref):
    h = h_ref[...]
    for c in range(4):
        res = _dot(h, w_ref[:, c * 512:(c + 1) * 512])
        for k in range(2):
            t = res[:, k * RET_DV:(k + 1) * RET_DV]
            if c >= 2:
                t = t * _sigmoid(t)
            o_ref[0, c * 2 + k] = t.astype(o_ref.dtype)


def _proj_dil_kernel(h_ref, w_ref, qn_ref, kn_ref, o_ref, *scr, d):
    h = h_ref[...]
    qn = qn_ref[...] * (DIL_DH ** -0.5)
    kn = kn_ref[...]
    tm = h.shape[0]
    for c in range(3):
        res = _dot(h, w_ref[:, c * 512:(c + 1) * 512])
        for k in range(DIL_GROUP_HEADS):
            t = res[:, k * DIL_DH:(k + 1) * DIL_DH]
            if c == 0:
                t = _rms(t, qn)
            elif c == 1:
                t = _rms(t, kn)
            hd = c * DIL_GROUP_HEADS + k
            if d == 1:
                o_ref[0, hd, 0] = t.astype(o_ref.dtype)
            else:
                scr[0][hd] = t
    if d > 1:
        for hd in range(3 * DIL_GROUP_HEADS):
            for r in range(d):
                o_ref[0, hd, r] = scr[0][hd, pl.ds(r, tm // d, stride=d), :].astype(o_ref.dtype)


def _proj_gate_kernel(h_ref, w_ref, b_ref, o_ref):
    h = h_ref[...]
    n = w_ref.shape[1]
    for c0 in range(0, n, 512):
        res = _dot(h, w_ref[:, c0:c0 + 512]) + b_ref[:, c0:c0 + 512]
        o_ref[:, c0:c0 + 512] = _sigmoid(res).astype(o_ref.dtype)


def _proj_call(kernel, h, w, extra, extra_specs, out_shape, out_spec, scratch=(), name="proj", vmem_extra=0):
    T, D = h.shape
    tm = TOKEN_TILE
    n = w.shape[1]
    vmem = D * n * 2 + 2 * tm * D * 2 + 4 * tm * n * 2 + 6 * tm * 512 * 4 + vmem_extra
    return pl.pallas_call(
        kernel,
        grid=(T // tm,),
        in_specs=[pl.BlockSpec((tm, D), lambda i: (i, 0)),
                  pl.BlockSpec((D, n), lambda i: (0, 0))] + list(extra_specs),
        out_specs=out_spec,
        out_shape=out_shape,
        scratch_shapes=list(scratch),
        compiler_params=_cparams(vmem, 1),
        name=name,
    )(h, w, *extra)


def _ret_kernel(cdec_ref, q_ref, k_ref, v_ref, sg_ref, dec_ref, ng_ref, o_ref,
                kvf_scr, kvb_scr, fst_scr, bst_scr, *, nchunks):
    C = RET_CHUNK
    hd = pl.program_id(1)
    cf = cdec_ref[hd, 0]
    cb = cdec_ref[hd, 1]

    def rows(n):
        return pl.ds(pl.multiple_of(n * C, C), C)

    def kv_body(n, carry):
        kn = k_ref[0, 0, rows(n), :].astype(F32)
        vn = v_ref[0, 0, rows(n), :]
        kvf_scr[n] = _dot_tn((kn * dec_ref[0, 3]).astype(BF16), vn)
        kvb_scr[n] = _dot_tn((kn * dec_ref[0, 4]).astype(BF16), vn)
        return carry

    lax.fori_loop(0, nchunks, kv_body, 0, unroll=2)

    def scan_f(n, st):
        fst_scr[n] = st.astype(BF16)
        return cf * st + kvf_scr[n]

    lax.fori_loop(0, nchunks, scan_f, jnp.zeros((RET_DK, RET_DV), F32))

    def scan_b(i, st):
        n = nchunks - 1 - i
        bst_scr[n] = st.astype(BF16)
        return cb * st + kvb_scr[n]

    lax.fori_loop(0, nchunks, scan_b, jnp.zeros((RET_DK, RET_DV), F32))

    def out_body(n, carry):
        qn = q_ref[0, 0, rows(n), :]
        kn = k_ref[0, 0, rows(n), :]
        vn = v_ref[0, 0, rows(n), :]
        q32 = qn.astype(F32)
        s = _dot_nt(qn, kn) * dec_ref[0, 0]
        o = (_dot(s.astype(BF16), vn)
             + _dot((q32 * dec_ref[0, 1]).astype(BF16), fst_scr[n])
             + _dot((q32 * dec_ref[0, 2]).astype(BF16), bst_scr[n]))
        r = lax.rsqrt(jnp.mean(o * o, axis=-1, keepdims=True) + EPS)
        y = o * r * ng_ref[0] * sg_ref[0, 0, rows(n), :].astype(F32)
        o_ref[0, rows(n), :] = y.astype(o_ref.dtype)
        return carry

    lax.fori_loop(0, nchunks, out_body, 0, unroll=2)


def _retention(qk, vg, cdec, dec, ng):
    B, _, S, _ = qk.shape
    H = RET_HEADS
    nchunks = S // RET_CHUNK
    vmem = (2 * (2 * S * RET_DK * 2 + 2 * S * RET_DV * 2 + S * RET_DV * 2)
            + nchunks * RET_DK * RET_DV * (4 + 4 + 2 + 2) + (8 << 20))
    return pl.pallas_call(
        functools.partial(_ret_kernel, nchunks=nchunks),
        grid=(B, H),
        in_specs=[pl.BlockSpec(memory_space=pltpu.SMEM),
                  pl.BlockSpec((1, 1, S, RET_DK), lambda b, h: (b, h, 0, 0)),
                  pl.BlockSpec((1, 1, S, RET_DK), lambda b, h: (b, H + h, 0, 0)),
                  pl.BlockSpec((1, 1, S, RET_DV), lambda b, h: (b, h, 0, 0)),
                  pl.BlockSpec((1, 1, S, RET_DV), lambda b, h: (b, H + h, 0, 0)),
                  pl.BlockSpec((1, 5, RET_CHUNK, RET_CHUNK), lambda b, h: (h, 0, 0, 0)),
                  pl.BlockSpec((1, 1, RET_DV), lambda b, h: (h, 0, 0))],
        out_specs=pl.BlockSpec((1, S, RET_DV), lambda b, h: (b, 0, h)),
        out_shape=jax.ShapeDtypeStruct((B, S, H * RET_DV), BF16),
        scratch_shapes=[pltpu.VMEM((nchunks, RET_DK, RET_DV), F32),
                        pltpu.VMEM((nchunks, RET_DK, RET_DV), F32),
                        pltpu.VMEM((nchunks, RET_DK, RET_DV), BF16),
                        pltpu.VMEM((nchunks, RET_DK, RET_DV), BF16)],
        compiler_params=_cparams(vmem, 2),
        name="retention",
    )(cdec, qk, qk, vg, vg, dec, ng)


def _dil_kernel(q0, k0, v0, q1, k1, v1, q2, k2, v2, bias_ref, o_ref, osc, lsc, *, cfgs, seq):
    refs = ((q0, k0, v0), (q1, k1, v1), (q2, k2, v2))
    QB, KW = DIL_QBLK, DIL_KWIN
    for g, (d, L) in enumerate(cfgs):
        qr, kr, vr = refs[g]
        nblk = L // QB

        def tile(idx, carry, g=g, d=d, L=L, qr=qr, kr=kr, vr=vr, nblk=nblk):
            r = lax.shift_right_logical(idx, nblk.bit_length() - 1)
            n = lax.bitwise_and(idx, nblk - 1)
            i0 = pl.multiple_of(n * QB, QB)
            w0 = pl.multiple_of(jnp.clip(i0 - DIL_R, 0, L - KW), DIL_R)
            var = jnp.where(n == 0, 0, jnp.where(n == nblk - 1, 2, 1))
            q = qr[0, 0, r, pl.ds(i0, QB), :]
            kw = kr[0, 0, r, pl.ds(w0, KW), :]
            vw = vr[0, 0, r, pl.ds(w0, KW), :]
            s = _dot_nt(q, kw) + bias_ref[0, g, var]
            m = jnp.max(s, axis=-1, keepdims=True)
            p = jnp.exp(s - m)
            l = jnp.sum(p, axis=-1, keepdims=True)
            o = _dot(p.astype(BF16), vw) / l
            lse = m + jnp.log(l)
            row0 = i0 * d + r
            sl = pl.ds(row0, QB) if d == 1 else pl.ds(row0, QB, stride=d)
            osc[g, sl, :] = o
            lsc[g, sl, :] = jnp.broadcast_to(lse, (QB, DIL_DH))
            return carry

        lax.fori_loop(0, seq // QB, tile, 0, unroll=4)

    CB = 256

    def comb(t, carry):
        rws = pl.ds(pl.multiple_of(t * CB, CB), CB)
        l0 = lsc[0, rws, :]
        l1 = lsc[1, rws, :]
        l2 = lsc[2, rws, :]
        mx = jnp.maximum(jnp.maximum(l0, l1), l2)
        e0 = jnp.exp(l0 - mx)
        e1 = jnp.exp(l1 - mx)
        e2 = jnp.exp(l2 - mx)
        num = e0 * osc[0, rws, :] + e1 * osc[1, rws, :] + e2 * osc[2, rws, :]
        o_ref[0, rws, :] = (num / (e0 + e1 + e2)).astype(o_ref.dtype)
        return carry

    lax.fori_loop(0, seq // CB, comb, 0)


def _dilated_attention(groups, bias):
    B = groups[0].shape[0]
    S = groups[0].shape[2] * groups[0].shape[3]
    G = DIL_GROUP_HEADS
    cfgs = tuple((a.shape[2], a.shape[3]) for a in groups)
    in_specs = []
    args = []
    for a, (d, L) in zip(groups, cfgs):
        for part in range(3):
            in_specs.append(pl.BlockSpec((1, 1, d, L, DIL_DH),
                                         lambda b, j, part=part: (b, part * G + j, 0, 0, 0)))
            args.append(a)
    in_specs.append(pl.BlockSpec((1, 3, 3, DIL_QBLK, DIL_KWIN), lambda b, j: (j, 0, 0, 0, 0)))
    args.append(bias)
    vmem = (2 * 9 * S * DIL_DH * 2 + 2 * 9 * DIL_QBLK * DIL_KWIN * 4 + 2 * S * DIL_DH * 2
            + 2 * 3 * S * DIL_DH * 4 + (8 << 20))
    return pl.pallas_call(
        functools.partial(_dil_kernel, cfgs=cfgs, seq=S),
        grid=(B, G),
        in_specs=in_specs,
        out_specs=pl.BlockSpec((1, S, DIL_DH), lambda b, j: (b, 0, j)),
        out_shape=jax.ShapeDtypeStruct((B, S, G * DIL_DH), BF16),
        scratch_shapes=[pltpu.VMEM((3, S, DIL_DH), F32),
                        pltpu.VMEM((3, S, DIL_DH), F32)],
        compiler_params=_cparams(vmem, 2),
        name="dilated_attention",
    )(*args)


def _merge_kernel(x_ref, yr_ref, yd_ref, gt_ref, wro_ref, wdo_ref, wout_ref, gn_ref, xo_ref, ho_ref):
    D = x_ref.shape[1]
    a = _dot(yr_ref[...], wro_ref[...])
    b = _dot(yd_ref[...], wdo_ref[...])
    mg = gt_ref[:, :D].astype(F32) * a + gt_ref[:, D:].astype(F32) * b
    xn = x_ref[...] + _dot(mg.astype(BF16), wout_ref[...])
    xo_ref[...] = xn
    ho_ref[...] = _rms(xn, gn_ref[...]).astype(ho_ref.dtype)


def _merge(x, yr, yd, gates, wro, wdo, wout, g_next):
    T, D = x.shape
    tm = TOKEN_TILE
    row = lambda n: pl.BlockSpec((tm, n), lambda i: (i, 0))
    full = lambda a: pl.BlockSpec(a.shape, lambda i: (0, 0))
    vmem = ((wro.size + wdo.size + wout.size) * 2
            + 2 * tm * (D * 4 + yr.shape[1] * 2 + yd.shape[1] * 2 + gates.shape[1] * 2 + D * 4 + D * 2)
            + 8 * tm * D * 4)
    return pl.pallas_call(
        _merge_kernel,
        grid=(T // tm,),
        in_specs=[row(D), row(yr.shape[1]), row(yd.shape[1]), row(gates.shape[1]),
                  full(wro), full(wdo), full(wout), pl.BlockSpec((1, D), lambda i: (0, 0))],
        out_specs=[row(D), row(D)],
        out_shape=[jax.ShapeDtypeStruct((T, D), F32), jax.ShapeDtypeStruct((T, D), BF16)],
        compiler_params=_cparams(vmem, 1),
        name="merge",
    )(x, yr, yd, gates, wro, wdo, wout, g_next.reshape(1, D))


def _t5_bucket(rel):
    nb = N_BUCKETS // 2
    max_exact = nb // 2
    ret = jnp.where(rel > 0, nb, 0)
    n = jnp.abs(rel)
    nf = jnp.maximum(n, 1).astype(F32)
    large = max_exact + (jnp.log(nf / max_exact) / math.log(MAX_DISTANCE / max_exact)
                         * (nb - max_exact)).astype(jnp.int32)
    large = jnp.minimum(large, nb - 1)
    return ret + jnp.where(n < max_exact, n, large)


def _dil_bias_tiles(rel_bias):
    a = jnp.arange(DIL_QBLK)[:, None]
    c = jnp.arange(DIL_KWIN)[None, :]
    per_group = []
    for gi, (_, d) in enumerate(DIL_CONFIGS):
        tiles = []
        for delta in (0, DIL_R, 2 * DIL_R):
            off = c - delta - a
            valid = jnp.abs(off) <= DIL_R
            bucket = _t5_bucket(jnp.clip(off, -DIL_R, DIL_R) * d)
            b = rel_bias[bucket][..., gi * DIL_GROUP_HEADS:(gi + 1) * DIL_GROUP_HEADS]
            tiles.append(jnp.where(valid[..., None], b.astype(F32), NEG))
        per_group.append(jnp.stack(tiles, axis=0))
    t = jnp.stack(per_group, axis=0)
    return jnp.transpose(t, (4, 0, 1, 2, 3))


def _ret_decay_tables(decay_exp):
    C = RET_CHUNK
    lg = jnp.log1p(-jnp.exp2(-decay_exp.astype(F32)))
    lf = lg[0][:, None, None]
    lb = lg[1][:, None, None]
    idx = jnp.arange(C, dtype=F32)
    diff = idx[:, None] - idx[None, :]
    dm = jnp.where(diff[None] >= 0, jnp.exp(lf * jnp.maximum(diff, 0.0)[None]),
                   jnp.exp(lb * jnp.maximum(-diff, 0.0)[None]))
    col = lambda v: jnp.broadcast_to(v[:, :, None], (v.shape[0], C, C))
    qdf = col(jnp.exp(lg[0][:, None] * (idx + 1.0)[None]))
    qdb = col(jnp.exp(lg[1][:, None] * (C - idx)[None]))
    kdf = col(jnp.exp(lg[0][:, None] * (C - 1.0 - idx)[None]))
    kdb = col(jnp.exp(lg[1][:, None] * idx[None]))
    dec = jnp.stack([dm, qdf, qdb, kdf, kdb], axis=1)
    cdec = jnp.stack([jnp.exp(lg[0] * C), jnp.exp(lg[1] * C)], axis=1)
    return dec, cdec


def _rope_tables(S):
    half = RET_DK // 2
    pos = jnp.arange(S, dtype=F32)
    inv = ROPE_BASE ** (-jnp.arange(half, dtype=F32) / half)
    ang = pos[:, None] * inv[None, :]
    cos, sin = jnp.cos(ang), jnp.sin(ang)
    return jnp.concatenate([cos, cos], axis=-1), jnp.concatenate([-sin, sin], axis=-1)


def kernel(x, rel_bias, norm_ffn1, ffn1_gate, ffn1_up, ffn1_down, norm_mix, w_in, b_gate,
           ret_decay_exp, ret_norm, w_ret_o, dil_q_norm, dil_k_norm, w_dil_o, w_out, norm_ffn2,
           ffn2_gate, ffn2_up, ffn2_down):
    B, S, D = x.shape
    depth = w_in.shape[0]
    T = B * S
    tm = TOKEN_TILE
    nS = S // tm
    assert S % tm == 0 and S % (16 * DIL_KWIN) == 0
    ret_qk = RET_HEADS * RET_DK
    ret_v = RET_HEADS * RET_DV
    dil_w = DIL_GROUP_HEADS * len(DIL_CONFIGS) * DIL_DH
    o_rq, o_rk, o_rv, o_rg = 0, ret_qk, 2 * ret_qk, 2 * ret_qk + ret_v
    o_dq = 2 * ret_qk + 2 * ret_v
    o_dk, o_dv = o_dq + dil_w, o_dq + 2 * dil_w
    o_gate = o_dq + 3 * dil_w

    cosf, sinf = _rope_tables(S)
    bias_tiles = _dil_bias_tiles(rel_bias)
    seq_idx = lambda i: (i // nS, 0, i % nS, 0)

    xf = x.reshape(T, D)
    h = _rmsnorm(xf, norm_ffn1[0])
    for l in range(depth):
        bf = lambda a: a.astype(BF16)
        xf, h = _ffn(xf, h, bf(ffn1_gate[l]), bf(ffn1_up[l]), bf(ffn1_down[l]), norm_mix[l])

        w = w_in[l]
        qk = _proj_call(
            functools.partial(_proj_retqk_kernel, kscale=RET_DK ** -0.5),
            h, bf(w[:, o_rq:o_rv]), (cosf, sinf),
            [pl.BlockSpec((tm, RET_DK), lambda i: (i % nS, 0))] * 2,
            jax.ShapeDtypeStruct((B, 2 * RET_HEADS, S, RET_DK), BF16),
            pl.BlockSpec((1, 2 * RET_HEADS, tm, RET_DK), seq_idx), name="proj_ret_qk")
        vg = _proj_call(
            _proj_retvg_kernel, h, bf(w[:, o_rv:o_dq]), (), [],
            jax.ShapeDtypeStruct((B, 2 * RET_HEADS, S, RET_DV), BF16),
            pl.BlockSpec((1, 2 * RET_HEADS, tm, RET_DV), seq_idx), name="proj_ret_vg")
        groups = []
        for gi, (_, d) in enumerate(DIL_CONFIGS):
            gw = DIL_GROUP_HEADS * DIL_DH
            wgrp = jnp.concatenate([w[:, o + gi * gw:o + (gi + 1) * gw] for o in (o_dq, o_dk, o_dv)], axis=1)
            scratch = [] if d == 1 else [pltpu.VMEM((3 * DIL_GROUP_HEADS, tm, DIL_DH), F32)]
            groups.append(_proj_call(
                functools.partial(_proj_dil_kernel, d=d),
                h, bf(wgrp), (dil_q_norm[l].reshape(1, DIL_DH), dil_k_norm[l].reshape(1, DIL_DH)),
                [pl.BlockSpec((1, DIL_DH), lambda i: (0, 0))] * 2,
                jax.ShapeDtypeStruct((B, 3 * DIL_GROUP_HEADS, d, S // d, DIL_DH), BF16),
                pl.BlockSpec((1, 3 * DIL_GROUP_HEADS, d, tm // d, DIL_DH),
                             lambda i: (i // nS, 0, 0, i % nS, 0)),
                scratch=scratch, name=f"proj_dil{gi}", vmem_extra=3 * DIL_GROUP_HEADS * tm * DIL_DH * 4))
        gates = _proj_call(
            _proj_gate_kernel, h, bf(w[:, o_gate:]), (b_gate[l].reshape(1, 2 * D),),
            [pl.BlockSpec((1, 2 * D), lambda i: (0, 0))],
            jax.ShapeDtypeStruct((T, 2 * D), BF16),
            pl.BlockSpec((tm, 2 * D), lambda i: (i, 0)), name="proj_gates")

        dec, cdec = _ret_decay_tables(ret_decay_exp[l])
        yr = _retention(qk, vg, cdec, dec, ret_norm[l].reshape(RET_HEADS, 1, RET_DV))
        yd = _dilated_attention(groups, bias_tiles)

        xf, h = _merge(xf, yr.reshape(T, ret_v), yd.reshape(T, DIL_GROUP_HEADS * DIL_DH), gates,
                       bf(w_ret_o[l]), bf(w_dil_o[l]), bf(w_out[l]), norm_ffn2[l])
        g_next = norm_ffn1[l + 1] if l + 1 < depth else norm_ffn1[0]
        xf, h = _ffn(xf, h, bf(ffn2_gate[l]), bf(ffn2_up[l]), bf(ffn2_down[l]), g_next)
    return xf.reshape(B, S, D)
```

```python
import functools
import math

import jax
import jax.numpy as jnp
from jax import lax
from jax.experimental import pallas as pl
from jax.experimental.pallas import tpu as pltpu

F32 = jnp.float32
BF16 = jnp.bfloat16

V7X_LANES = 128
V7X_VMEM_BYTES = 64 * 1024 * 1024

EPS = 1e-6
ROPE_BASE = 10000.0
NEG = -1e30

RET_HEADS = 4
RET_DK = 128
RET_DV = 256
RET_CHUNK = 128

DIL_CONFIGS = ((128, 1), (512, 4), (2048, 16))
DIL_GROUP_HEADS = 4
DIL_DH = 128
DIL_R = 64
DIL_QBLK = 2 * DIL_R
DIL_KWIN = 4 * DIL_R
N_BUCKETS = 32
MAX_DISTANCE = 1024

TOKEN_TILE = 512
MERGE_TILE = 1024
FFN_CHUNK = 512
PROJ_CHUNK = 512


def _cparams(vmem_bytes, ngrid):
    return pltpu.CompilerParams(
        dimension_semantics=("arbitrary",) * ngrid,
        vmem_limit_bytes=int(min(vmem_bytes, V7X_VMEM_BYTES - (4 << 20))),
    )


def _dot(a, b):
    return jnp.dot(a, b, preferred_element_type=F32)


def _dot_nt(a, b):
    return lax.dot_general(a, b, (((1,), (1,)), ((), ())), preferred_element_type=F32)


def _dot_tn(a, b):
    return lax.dot_general(a, b, (((0,), (0,)), ((), ())), preferred_element_type=F32)


def _rms(x32, g):
    ms = jnp.mean(x32 * x32, axis=-1, keepdims=True)
    return x32 * lax.rsqrt(ms + EPS) * g


def _sigmoid(x):
    return 1.0 / (1.0 + jnp.exp(-x))


def _norm_kernel(x_ref, g_ref, o_ref):
    o_ref[...] = _rms(x_ref[...], g_ref[...]).astype(o_ref.dtype)


def _rmsnorm(x, g):
    T, D = x.shape
    tm = TOKEN_TILE
    return pl.pallas_call(
        _norm_kernel,
        grid=(T // tm,),
        in_specs=[pl.BlockSpec((tm, D), lambda i: (i, 0)),
                  pl.BlockSpec((1, D), lambda i: (0, 0))],
        out_specs=pl.BlockSpec((tm, D), lambda i: (i, 0)),
        out_shape=jax.ShapeDtypeStruct((T, D), BF16),
        compiler_params=_cparams(16 << 20, 1),
        name="rmsnorm",
    )(x, g.reshape(1, D))


def _ffn_kernel(x_ref, h_ref, wg_ref, wu_ref, wd_ref, *rest, chunks):
    h = h_ref[...]
    acc = None
    for c0, cw in chunks:
        g = _dot(h, wg_ref[:, c0:c0 + cw])
        u = _dot(h, wu_ref[:, c0:c0 + cw])
        a = (g * _sigmoid(g) * u).astype(BF16)
        d = _dot(a, wd_ref[c0:c0 + cw, :])
        acc = d if acc is None else acc + d
    xn = x_ref[...] + 0.5 * acc
    if len(rest) == 3:
        gn_ref, xo_ref, ho_ref = rest
        ho_ref[...] = _rms(xn, gn_ref[...]).astype(ho_ref.dtype)
    else:
        xo_ref, = rest
    xo_ref[...] = xn


def _ffn(x, h, wg, wu, wd, g_next=None):
    T, D = x.shape
    Fh = wg.shape[1]
    tm = TOKEN_TILE
    chunks = tuple((c0, min(FFN_CHUNK, Fh - c0)) for c0 in range(0, Fh, FFN_CHUNK))
    vmem = 3 * D * Fh * 2 + 2 * tm * D * (4 + 2 + 4 + 2) + 8 * tm * D * 4
    row = pl.BlockSpec((tm, D), lambda i: (i, 0))
    in_specs = [row, row,
                pl.BlockSpec((D, Fh), lambda i: (0, 0)),
                pl.BlockSpec((D, Fh), lambda i: (0, 0)),
                pl.BlockSpec((Fh, D), lambda i: (0, 0))]
    args = [x, h, wg, wu, wd]
    out_specs = [row]
    out_shape = [jax.ShapeDtypeStruct((T, D), F32)]
    if g_next is not None:
        in_specs.append(pl.BlockSpec((1, D), lambda i: (0, 0)))
        args.append(g_next.reshape(1, D))
        out_specs.append(row)
        out_shape.append(jax.ShapeDtypeStruct((T, D), BF16))
    return pl.pallas_call(
        functools.partial(_ffn_kernel, chunks=chunks),
        grid=(T // tm,),
        in_specs=in_specs,
        out_specs=out_specs,
        out_shape=out_shape,
        compiler_params=_cparams(vmem, 1),
        name="ffn",
    )(*args)


def _retqk_body(h, w_ref, cos_ref, sin_ref, o_ref):
    cos = cos_ref[...]
    sin = sin_ref[...]
    per = PROJ_CHUNK // RET_DK
    for c in range(w_ref.shape[1] // PROJ_CHUNK):
        res = _dot(h, w_ref[:, c * PROJ_CHUNK:(c + 1) * PROJ_CHUNK])
        for k in range(per):
            t = res[:, k * RET_DK:(k + 1) * RET_DK]
            r = t * cos + pltpu.roll(t, RET_DK // 2, axis=1) * sin
            hd = c * per + k
            if hd >= RET_HEADS:
                r = r * (RET_DK ** -0.5)
            o_ref[0, hd] = r.astype(o_ref.dtype)


def _retvg_body(h, w_ref, o_ref):
    per = PROJ_CHUNK // RET_DV
    for c in range(w_ref.shape[1] // PROJ_CHUNK):
        res = _dot(h, w_ref[:, c * PROJ_CHUNK:(c + 1) * PROJ_CHUNK])
        for k in range(per):
            t = res[:, k * RET_DV:(k + 1) * RET_DV]
            hd = c * per + k
            if hd >= RET_HEADS:
                t = t * _sigmoid(t)
            o_ref[0, hd] = t.astype(o_ref.dtype)


def _dil_body(h, w_ref, qn, kn, o_ref, scr_ref, d):
    tm = h.shape[0]
    for c in range(3):
        res = _dot(h, w_ref[:, c * PROJ_CHUNK:(c + 1) * PROJ_CHUNK])
        for k in range(DIL_GROUP_HEADS):
            t = res[:, k * DIL_DH:(k + 1) * DIL_DH]
            if c == 0:
                t = _rms(t, qn)
            elif c == 1:
                t = _rms(t, kn)
            hd = c * DIL_GROUP_HEADS + k
            if d == 1:
                o_ref[0, hd, 0] = t.astype(o_ref.dtype)
            else:
                scr_ref[hd] = t
    if d > 1:
        for hd in range(3 * DIL_GROUP_HEADS):
            for r in range(d):
                o_ref[0, hd, r] = scr_ref[hd, pl.ds(r, tm // d, stride=d), :].astype(o_ref.dtype)


def _gate_body(h, w_ref, b_ref, o_ref):
    for c0 in range(0, w_ref.shape[1], PROJ_CHUNK):
        res = _dot(h, w_ref[:, c0:c0 + PROJ_CHUNK]) + b_ref[:, c0:c0 + PROJ_CHUNK]
        o_ref[:, c0:c0 + PROJ_CHUNK] = _sigmoid(res).astype(o_ref.dtype)


def _proj_kernel(h_ref, wqk_ref, wvg_ref, wd0_ref, wd1_ref, wd2_ref, wgt_ref, cos_ref, sin_ref,
                 qn_ref, kn_ref, bg_ref, oqk_ref, ovg_ref, od0_ref, od1_ref, od2_ref, ogt_ref, scr_ref):
    h = h_ref[...]
    _retqk_body(h, wqk_ref, cos_ref, sin_ref, oqk_ref)
    _retvg_body(h, wvg_ref, ovg_ref)
    qn = qn_ref[...] * (DIL_DH ** -0.5)
    kn = kn_ref[...]
    for w_ref, o_ref, (_, d) in zip((wd0_ref, wd1_ref, wd2_ref), (od0_ref, od1_ref, od2_ref), DIL_CONFIGS):
        _dil_body(h, w_ref, qn, kn, o_ref, scr_ref, d)
    _gate_body(h, wgt_ref, bg_ref, ogt_ref)


def _projections(h, B, S, wqk, wvg, wdil, wgt, cosf, sinf, qn, kn, bg):
    T, D = h.shape
    tm = TOKEN_TILE
    nS = S // tm
    nh_dil = 3 * DIL_GROUP_HEADS
    full = lambda a: pl.BlockSpec(a.shape, lambda i: (0,) * a.ndim)
    seq_idx = lambda i: (i // nS, 0, i % nS, 0)
    in_specs = ([pl.BlockSpec((tm, D), lambda i: (i, 0)), full(wqk), full(wvg)] + [full(w) for w in wdil]
                + [full(wgt),
                   pl.BlockSpec((tm, RET_DK), lambda i: (i % nS, 0)),
                   pl.BlockSpec((tm, RET_DK), lambda i: (i % nS, 0)),
                   full(qn), full(kn), full(bg)])
    out_shape = [jax.ShapeDtypeStruct((B, 2 * RET_HEADS, S, RET_DK), BF16),
                 jax.ShapeDtypeStruct((B, 2 * RET_HEADS, S, RET_DV), BF16)]
    out_specs = [pl.BlockSpec((1, 2 * RET_HEADS, tm, RET_DK), seq_idx),
                 pl.BlockSpec((1, 2 * RET_HEADS, tm, RET_DV), seq_idx)]
    for _, d in DIL_CONFIGS:
        out_shape.append(jax.ShapeDtypeStruct((B, nh_dil, d, S // d, DIL_DH), BF16))
        out_specs.append(pl.BlockSpec((1, nh_dil, d, tm // d, DIL_DH), lambda i: (i // nS, 0, 0, i % nS, 0)))
    out_shape.append(jax.ShapeDtypeStruct((T, wgt.shape[1]), BF16))
    out_specs.append(pl.BlockSpec((tm, wgt.shape[1]), lambda i: (i, 0)))
    n_all = wqk.shape[1] + wvg.shape[1] + sum(w.shape[1] for w in wdil) + wgt.shape[1]
    vmem = (D * n_all * 2 + 2 * tm * D * 2 + 2 * tm * n_all * 2 + nh_dil * tm * DIL_DH * 4
            + 8 * tm * PROJ_CHUNK * 4)
    return pl.pallas_call(
        _proj_kernel,
        grid=(T // tm,),
        in_specs=in_specs,
        out_specs=out_specs,
        out_shape=out_shape,
        scratch_shapes=[pltpu.VMEM((nh_dil, tm, DIL_DH), F32)],
        compiler_params=_cparams(vmem, 1),
        name="projections",
    )(h, wqk, wvg, *wdil, wgt, cosf, sinf, qn, kn, bg)


def _ret_kernel(cdec_ref, q_ref, k_ref, v_ref, sg_ref, dec_ref, ng_ref, o_ref,
                kvf_scr, kvb_scr, fst_scr, bst_scr, *, nchunks):
    C = RET_CHUNK
    hd = pl.program_id(1)
    cf = cdec_ref[hd, 0]
    cb = cdec_ref[hd, 1]

    def rows(n):
        return pl.ds(pl.multiple_of(n * C, C), C)

    def kv_body(n, carry):
        kn = k_ref[0, 0, rows(n), :].astype(F32)
        vn = v_ref[0, 0, rows(n), :]
        kvf_scr[n] = _dot_tn((kn * dec_ref[0, 3]).astype(BF16), vn)
        kvb_scr[n] = _dot_tn((kn * dec_ref[0, 4]).astype(BF16), vn)
        return carry

    lax.fori_loop(0, nchunks, kv_body, 0, unroll=16)

    def scan_f(n, st):
        fst_scr[n] = st.astype(BF16)
        return cf * st + kvf_scr[n]

    lax.fori_loop(0, nchunks, scan_f, jnp.zeros((RET_DK, RET_DV), F32))

    def scan_b(i, st):
        n = nchunks - 1 - i
        bst_scr[n] = st.astype(BF16)
        return cb * st + kvb_scr[n]

    lax.fori_loop(0, nchunks, scan_b, jnp.zeros((RET_DK, RET_DV), F32))

    def out_body(n, carry):
        qn = q_ref[0, 0, rows(n), :]
        kn = k_ref[0, 0, rows(n), :]
        vn = v_ref[0, 0, rows(n), :]
        q32 = qn.astype(F32)
        s = _dot_nt(qn, kn) * dec_ref[0, 0]
        o = (_dot(s.astype(BF16), vn)
             + _dot((q32 * dec_ref[0, 1]).astype(BF16), fst_scr[n])
             + _dot((q32 * dec_ref[0, 2]).astype(BF16), bst_scr[n]))
        r = lax.rsqrt(jnp.mean(o * o, axis=-1, keepdims=True) + EPS)
        y = o * r * ng_ref[0] * sg_ref[0, 0, rows(n), :].astype(F32)
        o_ref[0, rows(n), :] = y.astype(o_ref.dtype)
        return carry

    lax.fori_loop(0, nchunks, out_body, 0, unroll=16)


def _retention(qk, vg, cdec, dec, ng):
    B, _, S, _ = qk.shape
    H = RET_HEADS
    nchunks = S // RET_CHUNK
    vmem = (2 * (2 * S * RET_DK * 2 + 2 * S * RET_DV * 2 + S * RET_DV * 2)
            + nchunks * RET_DK * RET_DV * (4 + 4 + 2 + 2) + (8 << 20))
    return pl.pallas_call(
        functools.partial(_ret_kernel, nchunks=nchunks),
        grid=(B, H),
        in_specs=[pl.BlockSpec(memory_space=pltpu.SMEM),
                  pl.BlockSpec((1, 1, S, RET_DK), lambda b, h: (b, h, 0, 0)),
                  pl.BlockSpec((1, 1, S, RET_DK), lambda b, h: (b, H + h, 0, 0)),
                  pl.BlockSpec((1, 1, S, RET_DV), lambda b, h: (b, h, 0, 0)),
                  pl.BlockSpec((1, 1, S, RET_DV), lambda b, h: (b, H + h, 0, 0)),
                  pl.BlockSpec((1, 5, RET_CHUNK, RET_CHUNK), lambda b, h: (h, 0, 0, 0)),
                  pl.BlockSpec((1, 1, RET_DV), lambda b, h: (h, 0, 0))],
        out_specs=pl.BlockSpec((1, S, RET_DV), lambda b, h: (b, 0, h)),
        out_shape=jax.ShapeDtypeStruct((B, S, H * RET_DV), BF16),
        scratch_shapes=[pltpu.VMEM((nchunks, RET_DK, RET_DV), F32),
                        pltpu.VMEM((nchunks, RET_DK, RET_DV), F32),
                        pltpu.VMEM((nchunks, RET_DK, RET_DV), BF16),
                        pltpu.VMEM((nchunks, RET_DK, RET_DV), BF16)],
        compiler_params=_cparams(vmem, 2),
        name="retention",
    )(cdec, qk, qk, vg, vg, dec, ng)


def _dil_kernel(q0, k0, v0, q1, k1, v1, q2, k2, v2, bias_ref, o_ref, osc, lsc, *, cfgs, seq):
    refs = ((q0, k0, v0), (q1, k1, v1), (q2, k2, v2))
    QB, KW = DIL_QBLK, DIL_KWIN
    for g, (d, L) in enumerate(cfgs):
        qr, kr, vr = refs[g]
        nblk = L // QB

        def tile(idx, carry, g=g, d=d, L=L, qr=qr, kr=kr, vr=vr, nblk=nblk):
            r = lax.shift_right_logical(idx, nblk.bit_length() - 1)
            n = lax.bitwise_and(idx, nblk - 1)
            i0 = pl.multiple_of(n * QB, QB)
            w0 = pl.multiple_of(jnp.clip(i0 - DIL_R, 0, L - KW), DIL_R)
            var = jnp.where(n == 0, 0, jnp.where(n == nblk - 1, 2, 1))
            q = qr[0, 0, r, pl.ds(i0, QB), :]
            kw = kr[0, 0, r, pl.ds(w0, KW), :]
            vw = vr[0, 0, r, pl.ds(w0, KW), :]
            s = _dot_nt(q, kw) + bias_ref[0, g, var]
            m = jnp.max(s, axis=-1, keepdims=True)
            p = jnp.exp(s - m)
            l = jnp.sum(p, axis=-1, keepdims=True)
            o = _dot(p.astype(BF16), vw) / l
            lse = m + jnp.log(l)
            row0 = i0 * d + r
            sl = pl.ds(row0, QB) if d == 1 else pl.ds(row0, QB, stride=d)
            osc[g, sl, :] = o
            lsc[g, sl, :] = jnp.broadcast_to(lse, (QB, DIL_DH))
            return carry

        lax.fori_loop(0, seq // QB, tile, 0, unroll=16)

    CB = 256

    def comb(t, carry):
        rws = pl.ds(pl.multiple_of(t * CB, CB), CB)
        l0 = lsc[0, rws, :]
        l1 = lsc[1, rws, :]
        l2 = lsc[2, rws, :]
        mx = jnp.maximum(jnp.maximum(l0, l1), l2)
        e0 = jnp.exp(l0 - mx)
        e1 = jnp.exp(l1 - mx)
        e2 = jnp.exp(l2 - mx)
        num = e0 * osc[0, rws, :] + e1 * osc[1, rws, :] + e2 * osc[2, rws, :]
        o_ref[0, rws, :] = (num / (e0 + e1 + e2)).astype(o_ref.dtype)
        return carry

    lax.fori_loop(0, seq // CB, comb, 0, unroll=2)


def _dilated_attention(groups, bias):
    B = groups[0].shape[0]
    S = groups[0].shape[2] * groups[0].shape[3]
    G = DIL_GROUP_HEADS
    cfgs = tuple((a.shape[2], a.shape[3]) for a in groups)
    in_specs = []
    args = []
    for a, (d, L) in zip(groups, cfgs):
        for part in range(3):
            in_specs.append(pl.BlockSpec((1, 1, d, L, DIL_DH),
                                         lambda b, j, part=part: (b, part * G + j, 0, 0, 0)))
            args.append(a)
    in_specs.append(pl.BlockSpec((1, 3, 3, DIL_QBLK, DIL_KWIN), lambda b, j: (j, 0, 0, 0, 0)))
    args.append(bias)
    vmem = (2 * 9 * S * DIL_DH * 2 + 2 * 9 * DIL_QBLK * DIL_KWIN * 4 + 2 * S * DIL_DH * 2
            + 2 * 3 * S * DIL_DH * 4 + (8 << 20))
    return pl.pallas_call(
        functools.partial(_dil_kernel, cfgs=cfgs, seq=S),
        grid=(B, G),
        in_specs=in_specs,
        out_specs=pl.BlockSpec((1, S, DIL_DH), lambda b, j: (b, 0, j)),
        out_shape=jax.ShapeDtypeStruct((B, S, G * DIL_DH), BF16),
        scratch_shapes=[pltpu.VMEM((3, S, DIL_DH), F32),
                        pltpu.VMEM((3, S, DIL_DH), F32)],
        compiler_params=_cparams(vmem, 2),
        name="dilated_attention",
    )(*args)


def _merge_kernel(x_ref, yr_ref, yd_ref, gt_ref, wro_ref, wdo_ref, wout_ref, gn_ref, xo_ref, ho_ref):
    D = x_ref.shape[1]
    a = _dot(yr_ref[...], wro_ref[...])
    b = _dot(yd_ref[...], wdo_ref[...])
    mg = gt_ref[:, :D].astype(F32) * a + gt_ref[:, D:].astype(F32) * b
    xn = x_ref[...] + _dot(mg.astype(BF16), wout_ref[...])
    xo_ref[...] = xn
    ho_ref[...] = _rms(xn, gn_ref[...]).astype(ho_ref.dtype)


def _merge(x, yr, yd, gates, wro, wdo, wout, g_next):
    T, D = x.shape
    tm = MERGE_TILE
    row = lambda n: pl.BlockSpec((tm, n), lambda i: (i, 0))
    full = lambda a: pl.BlockSpec(a.shape, lambda i: (0, 0))
    vmem = ((wro.size + wdo.size + wout.size) * 2
            + 2 * tm * (D * 4 + yr.shape[1] * 2 + yd.shape[1] * 2 + gates.shape[1] * 2 + D * 4 + D * 2)
            + 6 * tm * D * 4)
    return pl.pallas_call(
        _merge_kernel,
        grid=(T // tm,),
        in_specs=[row(D), row(yr.shape[1]), row(yd.shape[1]), row(gates.shape[1]),
                  full(wro), full(wdo), full(wout), pl.BlockSpec((1, D), lambda i: (0, 0))],
        out_specs=[row(D), row(D)],
        out_shape=[jax.ShapeDtypeStruct((T, D), F32), jax.ShapeDtypeStruct((T, D), BF16)],
        compiler_params=_cparams(vmem, 1),
        name="merge",
    )(x, yr, yd, gates, wro, wdo, wout, g_next.reshape(1, D))


def _t5_bucket(rel):
    nb = N_BUCKETS // 2
    max_exact = nb // 2
    ret = jnp.where(rel > 0, nb, 0)
    n = jnp.abs(rel)
    nf = jnp.maximum(n, 1).astype(F32)
    large = max_exact + (jnp.log(nf / max_exact) / math.log(MAX_DISTANCE / max_exact)
                         * (nb - max_exact)).astype(jnp.int32)
    large = jnp.minimum(large, nb - 1)
    return ret + jnp.where(n < max_exact, n, large)


def _dil_bias_tiles(rel_bias):
    a = jnp.arange(DIL_QBLK)[:, None]
    c = jnp.arange(DIL_KWIN)[None, :]
    per_group = []
    for gi, (_, d) in enumerate(DIL_CONFIGS):
        tiles = []
        for delta in (0, DIL_R, 2 * DIL_R):
            off = c - delta - a
            valid = jnp.abs(off) <= DIL_R
            bucket = _t5_bucket(jnp.clip(off, -DIL_R, DIL_R) * d)
            rb = rel_bias[:, gi * DIL_GROUP_HEADS:(gi + 1) * DIL_GROUP_HEADS].astype(F32)
            hit = bucket[..., None] == jnp.arange(N_BUCKETS)
            b = jnp.sum(jnp.where(hit[..., None], rb[None, None], 0.0), axis=2)
            tiles.append(jnp.where(valid[..., None], b, NEG))
        per_group.append(jnp.stack(tiles, axis=0))
    t = jnp.stack(per_group, axis=0)
    return jnp.transpose(t, (4, 0, 1, 2, 3))


def _ret_decay_tables(decay_exp):
    C = RET_CHUNK
    lg = jnp.log1p(-jnp.exp2(-decay_exp.astype(F32)))
    lf = lg[0][:, None, None]
    lb = lg[1][:, None, None]
    idx = jnp.arange(C, dtype=F32)
    diff = idx[:, None] - idx[None, :]
    dm = jnp.where(diff[None] >= 0, jnp.exp(lf * jnp.maximum(diff, 0.0)[None]),
                   jnp.exp(lb * jnp.maximum(-diff, 0.0)[None]))
    col = lambda v: jnp.broadcast_to(v[:, :, None], (v.shape[0], C, C))
    qdf = col(jnp.exp(lg[0][:, None] * (idx + 1.0)[None]))
    qdb = col(jnp.exp(lg[1][:, None] * (C - idx)[None]))
    kdf = col(jnp.exp(lg[0][:, None] * (C - 1.0 - idx)[None]))
    kdb = col(jnp.exp(lg[1][:, None] * idx[None]))
    dec = jnp.stack([dm, qdf, qdb, kdf, kdb], axis=1)
    cdec = jnp.stack([jnp.exp(lg[0] * C), jnp.exp(lg[1] * C)], axis=1)
    return dec, cdec


def _rope_tables(S):
    half = RET_DK // 2
    pos = jnp.arange(S, dtype=F32)
    inv = ROPE_BASE ** (-jnp.arange(half, dtype=F32) / half)
    ang = pos[:, None] * inv[None, :]
    cos, sin = jnp.cos(ang), jnp.sin(ang)
    return jnp.concatenate([cos, cos], axis=-1), jnp.concatenate([-sin, sin], axis=-1)


def kernel(x, rel_bias, norm_ffn1, ffn1_gate, ffn1_up, ffn1_down, norm_mix, w_in, b_gate,
           ret_decay_exp, ret_norm, w_ret_o, dil_q_norm, dil_k_norm, w_dil_o, w_out, norm_ffn2,
           ffn2_gate, ffn2_up, ffn2_down):
    B, S, D = x.shape
    depth = w_in.shape[0]
    T = B * S
    assert S % TOKEN_TILE == 0 and T % MERGE_TILE == 0 and S % (16 * DIL_KWIN) == 0
    ret_qk = RET_HEADS * RET_DK
    ret_v = RET_HEADS * RET_DV
    dil_w = DIL_GROUP_HEADS * len(DIL_CONFIGS) * DIL_DH
    o_rq, o_rv = 0, 2 * ret_qk
    o_dq = 2 * ret_qk + 2 * ret_v
    o_dk, o_dv = o_dq + dil_w, o_dq + 2 * dil_w
    o_gate = o_dq + 3 * dil_w
    gw = DIL_GROUP_HEADS * DIL_DH

    cosf, sinf = _rope_tables(S)
    bias_tiles = _dil_bias_tiles(rel_bias)
    bf = lambda a: a.astype(BF16)

    xf = x.reshape(T, D)
    h = _rmsnorm(xf, norm_ffn1[0])
    for l in range(depth):
        xf, h = _ffn(xf, h, bf(ffn1_gate[l]), bf(ffn1_up[l]), bf(ffn1_down[l]), norm_mix[l])

        w = w_in[l]
        wdil = [bf(jnp.concatenate([w[:, o + gi * gw:o + (gi + 1) * gw] for o in (o_dq, o_dk, o_dv)], axis=1))
                for gi in range(len(DIL_CONFIGS))]
        qk, vg, g0, g1, g2, gates = _projections(
            h, B, S, bf(w[:, o_rq:o_rv]), bf(w[:, o_rv:o_dq]), wdil, bf(w[:, o_gate:]), cosf, sinf,
            dil_q_norm[l].reshape(1, DIL_DH), dil_k_norm[l].reshape(1, DIL_DH), b_gate[l].reshape(1, 2 * D))

        dec, cdec = _ret_decay_tables(ret_decay_exp[l])
        yr = _retention(qk, vg, cdec, dec, ret_norm[l].reshape(RET_HEADS, 1, RET_DV))
        yd = _dilated_attention([g0, g1, g2], bias_tiles)

        xf, h = _merge(xf, yr.reshape(T, ret_v), yd.reshape(T, gw), gates,
                       bf(w_ret_o[l]), bf(w_dil_o[l]), bf(w_out[l]), norm_ffn2[l])
        if l + 1 < depth:
            xf, h = _ffn(xf, h, bf(ffn2_gate[l]), bf(ffn2_up[l]), bf(ffn2_down[l]), norm_ffn1[l + 1])
        else:
            xf, = _ffn(xf, h, bf(ffn2_gate[l]), bf(ffn2_up[l]), bf(ffn2_down[l]))
    return xf.reshape(B, S, D)
```

```python
import functools
import math

import jax
import jax.numpy as jnp
from jax import lax
from jax.experimental import pallas as pl
from jax.experimental.pallas import tpu as pltpu

F32 = jnp.float32
BF16 = jnp.bfloat16

V7X_LANES = 128
V7X_VMEM_BYTES = 64 * 1024 * 1024

EPS = 1e-6
ROPE_BASE = 10000.0
NEG = -1e30

RET_HEADS = 4
RET_DK = 128
RET_DV = 256
RET_CHUNK = 128
RET_BATCH = 8

DIL_CONFIGS = ((128, 1), (512, 4), (2048, 16))
DIL_GROUP_HEADS = 4
DIL_DH = 128
DIL_R = 64
DIL_QBLK = 2 * DIL_R
DIL_KWIN = 4 * DIL_R
DIL_BATCH = 8
N_BUCKETS = 32
MAX_DISTANCE = 1024

FFN_TILE = 1024
FFN_SUB = 512
TOKEN_TILE = 512
MERGE_TILE = 1024
FFN_CHUNK = 512
PROJ_CHUNK = 512


def _cparams(vmem_bytes, ngrid):
    return pltpu.CompilerParams(
        dimension_semantics=("arbitrary",) * ngrid,
        vmem_limit_bytes=int(min(vmem_bytes, V7X_VMEM_BYTES - (4 << 20))),
    )


def _dot(a, b):
    return jnp.dot(a, b, preferred_element_type=F32)


def _dot_nt(a, b):
    return lax.dot_general(a, b, (((1,), (1,)), ((), ())), preferred_element_type=F32)


def _dot_tn(a, b):
    return lax.dot_general(a, b, (((0,), (0,)), ((), ())), preferred_element_type=F32)


def _rms(x32, g):
    ms = jnp.mean(x32 * x32, axis=-1, keepdims=True)
    return x32 * lax.rsqrt(ms + EPS) * g


def _sigmoid(x):
    return 1.0 / (1.0 + jnp.exp(-x))


def _ffn_kernel(x_ref, hg_ref, wg_ref, wu_ref, wd_ref, *rest, chunks, norm_in):
    for r0 in range(0, x_ref.shape[0], FFN_SUB):
        rows = slice(r0, r0 + FFN_SUB)
        h = _rms(x_ref[rows, :], hg_ref[...]).astype(BF16) if norm_in else hg_ref[rows, :]
        acc = None
        for c0, cw in chunks:
            g = _dot(h, wg_ref[:, c0:c0 + cw])
            u = _dot(h, wu_ref[:, c0:c0 + cw])
            a = (g * _sigmoid(g) * u).astype(BF16)
            d = _dot(a, wd_ref[c0:c0 + cw, :])
            acc = d if acc is None else acc + d
        xn = x_ref[rows, :] + 0.5 * acc
        if len(rest) == 3:
            gn_ref, xo_ref, ho_ref = rest
            ho_ref[rows, :] = _rms(xn, gn_ref[...]).astype(ho_ref.dtype)
        else:
            xo_ref, = rest
        xo_ref[rows, :] = xn


def _ffn(x, h, wg, wu, wd, g_in=None, g_next=None):
    T, D = x.shape
    Fh = wg.shape[1]
    tm = FFN_TILE
    chunks = tuple((c0, min(FFN_CHUNK, Fh - c0)) for c0 in range(0, Fh, FFN_CHUNK))
    vmem = 3 * D * Fh * 2 + 2 * tm * D * (4 + 2 + 4 + 2) + 4 * tm * D * 4
    row = pl.BlockSpec((tm, D), lambda i: (i, 0))
    vec = pl.BlockSpec((1, D), lambda i: (0, 0))
    in_specs = [row, vec if h is None else row,
                pl.BlockSpec((D, Fh), lambda i: (0, 0)),
                pl.BlockSpec((D, Fh), lambda i: (0, 0)),
                pl.BlockSpec((Fh, D), lambda i: (0, 0))]
    args = [x, g_in.reshape(1, D) if h is None else h, wg, wu, wd]
    out_specs = [row]
    out_shape = [jax.ShapeDtypeStruct((T, D), F32)]
    if g_next is not None:
        in_specs.append(vec)
        args.append(g_next.reshape(1, D))
        out_specs.append(row)
        out_shape.append(jax.ShapeDtypeStruct((T, D), BF16))
    return pl.pallas_call(
        functools.partial(_ffn_kernel, chunks=chunks, norm_in=h is None),
        grid=(T // tm,),
        in_specs=in_specs,
        out_specs=out_specs,
        out_shape=out_shape,
        compiler_params=_cparams(vmem, 1),
        name="ffn",
    )(*args)


def _retqk_body(h_ref, w_ref, cos_ref, sin_ref, o_ref):
    cos = cos_ref[...]
    sin = sin_ref[...]
    per = PROJ_CHUNK // RET_DK
    for c in range(w_ref.shape[1] // PROJ_CHUNK):
        res = _dot(h_ref[...], w_ref[:, c * PROJ_CHUNK:(c + 1) * PROJ_CHUNK])
        for k in range(per):
            t = res[:, k * RET_DK:(k + 1) * RET_DK]
            r = t * cos + pltpu.roll(t, RET_DK // 2, axis=1) * sin
            hd = c * per + k
            if hd >= RET_HEADS:
                r = r * (RET_DK ** -0.5)
            o_ref[0, hd] = r.astype(o_ref.dtype)


def _retvg_body(h_ref, w_ref, o_ref):
    per = PROJ_CHUNK // RET_DV
    for c in range(w_ref.shape[1] // PROJ_CHUNK):
        res = _dot(h_ref[...], w_ref[:, c * PROJ_CHUNK:(c + 1) * PROJ_CHUNK])
        for k in range(per):
            t = res[:, k * RET_DV:(k + 1) * RET_DV]
            hd = c * per + k
            if hd >= RET_HEADS:
                t = t * _sigmoid(t)
            o_ref[0, hd] = t.astype(o_ref.dtype)


def _dil_body(h_ref, w_ref, qn, kn, o_ref, scr_ref, d):
    tm = h_ref.shape[0]
    for c in range(3):
        res = _dot(h_ref[...], w_ref[:, c * PROJ_CHUNK:(c + 1) * PROJ_CHUNK])
        for k in range(DIL_GROUP_HEADS):
            t = res[:, k * DIL_DH:(k + 1) * DIL_DH]
            if c == 0:
                t = _rms(t, qn)
            elif c == 1:
                t = _rms(t, kn)
            hd = c * DIL_GROUP_HEADS + k
            if d == 1:
                o_ref[0, hd, 0] = t.astype(o_ref.dtype)
            else:
                scr_ref[hd] = t
    if d > 1:
        for hd in range(3 * DIL_GROUP_HEADS):
            for r in range(d):
                o_ref[0, hd, r] = scr_ref[hd, pl.ds(r, tm // d, stride=d), :].astype(o_ref.dtype)


def _gate_body(h_ref, w_ref, b_ref, o_ref):
    for c0 in range(0, w_ref.shape[1], PROJ_CHUNK):
        res = _dot(h_ref[...], w_ref[:, c0:c0 + PROJ_CHUNK]) + b_ref[:, c0:c0 + PROJ_CHUNK]
        o_ref[:, c0:c0 + PROJ_CHUNK] = _sigmoid(res).astype(o_ref.dtype)


def _proj_kernel(h_ref, wqk_ref, wvg_ref, wd0_ref, wd1_ref, wd2_ref, wgt_ref, cos_ref, sin_ref,
                 qn_ref, kn_ref, bg_ref, oqk_ref, ovg_ref, od0_ref, od1_ref, od2_ref, ogt_ref, scr_ref):
    _retqk_body(h_ref, wqk_ref, cos_ref, sin_ref, oqk_ref)
    _retvg_body(h_ref, wvg_ref, ovg_ref)
    qn = qn_ref[...] * (DIL_DH ** -0.5)
    kn = kn_ref[...]
    for w_ref, o_ref, (_, d) in zip((wd0_ref, wd1_ref, wd2_ref), (od0_ref, od1_ref, od2_ref), DIL_CONFIGS):
        _dil_body(h_ref, w_ref, qn, kn, o_ref, scr_ref, d)
    _gate_body(h_ref, wgt_ref, bg_ref, ogt_ref)


def _projections(h, B, S, wqk, wvg, wdil, wgt, cosf, sinf, qn, kn, bg):
    T, D = h.shape
    tm = TOKEN_TILE
    nS = S // tm
    nh_dil = 3 * DIL_GROUP_HEADS
    full = lambda a: pl.BlockSpec(a.shape, lambda i: (0,) * a.ndim)
    seq_idx = lambda i: (i // nS, 0, i % nS, 0)
    in_specs = ([pl.BlockSpec((tm, D), lambda i: (i, 0)), full(wqk), full(wvg)] + [full(w) for w in wdil]
                + [full(wgt),
                   pl.BlockSpec((tm, RET_DK), lambda i: (i % nS, 0)),
                   pl.BlockSpec((tm, RET_DK), lambda i: (i % nS, 0)),
                   full(qn), full(kn), full(bg)])
    out_shape = [jax.ShapeDtypeStruct((B, 2 * RET_HEADS, S, RET_DK), BF16),
                 jax.ShapeDtypeStruct((B, 2 * RET_HEADS, S, RET_DV), BF16)]
    out_specs = [pl.BlockSpec((1, 2 * RET_HEADS, tm, RET_DK), seq_idx),
                 pl.BlockSpec((1, 2 * RET_HEADS, tm, RET_DV), seq_idx)]
    for _, d in DIL_CONFIGS:
        out_shape.append(jax.ShapeDtypeStruct((B, nh_dil, d, S // d, DIL_DH), BF16))
        out_specs.append(pl.BlockSpec((1, nh_dil, d, tm // d, DIL_DH), lambda i: (i // nS, 0, 0, i % nS, 0)))
    out_shape.append(jax.ShapeDtypeStruct((T, wgt.shape[1]), BF16))
    out_specs.append(pl.BlockSpec((tm, wgt.shape[1]), lambda i: (i, 0)))
    n_all = wqk.shape[1] + wvg.shape[1] + sum(w.shape[1] for w in wdil) + wgt.shape[1]
    vmem = (D * n_all * 2 + 2 * tm * D * 2 + 2 * tm * n_all * 2 + nh_dil * tm * DIL_DH * 4
            + 8 * tm * PROJ_CHUNK * 4)
    return pl.pallas_call(
        _proj_kernel,
        grid=(T // tm,),
        in_specs=in_specs,
        out_specs=out_specs,
        out_shape=out_shape,
        scratch_shapes=[pltpu.VMEM((nh_dil, tm, DIL_DH), F32)],
        compiler_params=_cparams(vmem, 1),
        name="projections",
    )(h, wqk, wvg, *wdil, wgt, cosf, sinf, qn, kn, bg)


def _ret_kernel(cdec_ref, q_ref, k_ref, v_ref, sg_ref, dec_ref, ng_ref, o_ref,
                kvf_scr, kvb_scr, fst_scr, bst_scr, *, nchunks):
    C = RET_CHUNK
    hd = pl.program_id(1)
    cf = cdec_ref[hd, 0]
    cb = cdec_ref[hd, 1]

    def rows(n):
        return pl.ds(pl.multiple_of(n * C, C), C)

    def kv_body(n, carry):
        kn = k_ref[0, 0, rows(n), :].astype(F32)
        vn = v_ref[0, 0, rows(n), :]
        kvf_scr[n] = _dot_tn((kn * dec_ref[0, 3]).astype(BF16), vn)
        kvb_scr[n] = _dot_tn((kn * dec_ref[0, 4]).astype(BF16), vn)
        return carry

    lax.fori_loop(0, nchunks, kv_body, 0, unroll=16)

    def scan_f(n, st):
        fst_scr[n] = st.astype(BF16)
        return cf * st + kvf_scr[n]

    lax.fori_loop(0, nchunks, scan_f, jnp.zeros((RET_DK, RET_DV), F32))

    def scan_b(i, st):
        n = nchunks - 1 - i
        bst_scr[n] = st.astype(BF16)
        return cb * st + kvb_scr[n]

    lax.fori_loop(0, nchunks, scan_b, jnp.zeros((RET_DK, RET_DV), F32))

    G = RET_BATCH

    def out_body(t, carry):
        rw = pl.ds(pl.multiple_of(t * (G * C), G * C), G * C)
        cs = pl.ds(pl.multiple_of(t * G, G), G)
        q = q_ref[0, 0, rw, :].reshape(G, C, RET_DK)
        k = k_ref[0, 0, rw, :].reshape(G, C, RET_DK)
        v = v_ref[0, 0, rw, :].reshape(G, C, RET_DV)
        q32 = q.astype(F32)
        s = jnp.einsum('gcd,gjd->gcj', q, k, preferred_element_type=F32) * dec_ref[0, 0]
        lhs = jnp.concatenate([s.astype(BF16), (q32 * dec_ref[0, 1]).astype(BF16),
                               (q32 * dec_ref[0, 2]).astype(BF16)], axis=2)
        rhs = jnp.concatenate([v, fst_scr[cs], bst_scr[cs]], axis=1)
        o = jnp.einsum('gck,gke->gce', lhs, rhs, preferred_element_type=F32)
        r = lax.rsqrt(jnp.mean(o * o, axis=-1, keepdims=True) + EPS)
        y = (o * r * ng_ref[0]).reshape(G * C, RET_DV) * sg_ref[0, 0, rw, :].astype(F32)
        o_ref[0, rw, :] = y.astype(o_ref.dtype)
        return carry

    lax.fori_loop(0, nchunks // G, out_body, 0, unroll=2)


def _retention(qk, vg, cdec, dec, ng):
    B, _, S, _ = qk.shape
    H = RET_HEADS
    nchunks = S // RET_CHUNK
    vmem = (2 * (2 * S * RET_DK * 2 + 2 * S * RET_DV * 2 + S * RET_DV * 2)
            + nchunks * RET_DK * RET_DV * (4 + 4 + 2 + 2) + (8 << 20))
    return pl.pallas_call(
        functools.partial(_ret_kernel, nchunks=nchunks),
        grid=(B, H),
        in_specs=[pl.BlockSpec(memory_space=pltpu.SMEM),
                  pl.BlockSpec((1, 1, S, RET_DK), lambda b, h: (b, h, 0, 0)),
                  pl.BlockSpec((1, 1, S, RET_DK), lambda b, h: (b, H + h, 0, 0)),
                  pl.BlockSpec((1, 1, S, RET_DV), lambda b, h: (b, h, 0, 0)),
                  pl.BlockSpec((1, 1, S, RET_DV), lambda b, h: (b, H + h, 0, 0)),
                  pl.BlockSpec((1, 5, RET_CHUNK, RET_CHUNK), lambda b, h: (h, 0, 0, 0)),
                  pl.BlockSpec((1, 1, RET_DV), lambda b, h: (h, 0, 0))],
        out_specs=pl.BlockSpec((1, S, RET_DV), lambda b, h: (b, 0, h)),
        out_shape=jax.ShapeDtypeStruct((B, S, H * RET_DV), BF16),
        scratch_shapes=[pltpu.VMEM((nchunks, RET_DK, RET_DV), F32),
                        pltpu.VMEM((nchunks, RET_DK, RET_DV), F32),
                        pltpu.VMEM((nchunks, RET_DK, RET_DV), BF16),
                        pltpu.VMEM((nchunks, RET_DK, RET_DV), BF16)],
        compiler_params=_cparams(vmem, 2),
        name="retention",
    )(cdec, qk, qk, vg, vg, dec, ng)


def _dil_kernel(q0, k0, v0, q1, k1, v1, q2, k2, v2, bias_ref, o_ref, osc, lsc, msc, *, cfgs, seq):
    refs = ((q0, k0, v0), (q1, k1, v1), (q2, k2, v2))
    QB, KW = DIL_QBLK, DIL_KWIN
    ones = jnp.ones((KW, DIL_DH), BF16)
    for g, (d, L) in enumerate(cfgs):
        qr, kr, vr = refs[g]
        nblk = L // QB

        def tiles(t, carry, g=g, d=d, L=L, qr=qr, kr=kr, vr=vr, nblk=nblk):
            qs, ks, vs, bs, sls = [], [], [], [], []
            for i in range(DIL_BATCH):
                idx = t * DIL_BATCH + i
                r = lax.shift_right_logical(idx, nblk.bit_length() - 1)
                n = lax.bitwise_and(idx, nblk - 1)
                i0 = pl.multiple_of(n * QB, QB)
                w0 = pl.multiple_of(jnp.clip(i0 - DIL_R, 0, L - KW), DIL_R)
                var = jnp.where(n == 0, 0, jnp.where(n == nblk - 1, 2, 1))
                qs.append(qr[0, 0, r, pl.ds(i0, QB), :])
                ks.append(kr[0, 0, r, pl.ds(w0, KW), :])
                vs.append(jnp.concatenate([vr[0, 0, r, pl.ds(w0, KW), :], ones], axis=1))
                bs.append(bias_ref[0, g, var])
                row0 = i0 * d + r
                sls.append(pl.ds(row0, QB) if d == 1 else pl.ds(row0, QB, stride=d))
            s = jnp.einsum('gqd,gkd->gqk', jnp.stack(qs), jnp.stack(ks),
                           preferred_element_type=F32) + jnp.stack(bs)
            m = jnp.max(s, axis=-1, keepdims=True)
            p = jnp.exp(s - m).astype(BF16)
            ol = jnp.einsum('gqk,gke->gqe', p, jnp.stack(vs), preferred_element_type=F32)
            for i, sl in enumerate(sls):
                osc[g, sl, :] = ol[i, :, :DIL_DH]
                lsc[g, sl, :] = ol[i, :, DIL_DH:]
                msc[g, sl, :] = jnp.broadcast_to(m[i], (QB, DIL_DH))
            return carry

        lax.fori_loop(0, seq // (QB * DIL_BATCH), tiles, 0, unroll=2)

    CB = 256

    def comb(t, carry):
        rws = pl.ds(pl.multiple_of(t * CB, CB), CB)
        m0 = msc[0, rws, :]
        m1 = msc[1, rws, :]
        m2 = msc[2, rws, :]
        mx = jnp.maximum(jnp.maximum(m0, m1), m2)
        e0 = jnp.exp(m0 - mx)
        e1 = jnp.exp(m1 - mx)
        e2 = jnp.exp(m2 - mx)
        num = e0 * osc[0, rws, :] + e1 * osc[1, rws, :] + e2 * osc[2, rws, :]
        den = e0 * lsc[0, rws, :] + e1 * lsc[1, rws, :] + e2 * lsc[2, rws, :]
        o_ref[0, rws, :] = (num / den).astype(o_ref.dtype)
        return carry

    lax.fori_loop(0, seq // CB, comb, 0, unroll=2)


def _dilated_attention(groups, bias):
    B = groups[0].shape[0]
    S = groups[0].shape[2] * groups[0].shape[3]
    G = DIL_GROUP_HEADS
    cfgs = tuple((a.shape[2], a.shape[3]) for a in groups)
    in_specs = []
    args = []
    for a, (d, L) in zip(groups, cfgs):
        for part in range(3):
            in_specs.append(pl.BlockSpec((1, 1, d, L, DIL_DH),
                                         lambda b, j, part=part: (b, part * G + j, 0, 0, 0)))
            args.append(a)
    in_specs.append(pl.BlockSpec((1, 3, 3, DIL_QBLK, DIL_KWIN), lambda b, j: (j, 0, 0, 0, 0)))
    args.append(bias)
    vmem = (2 * 9 * S * DIL_DH * 2 + 2 * 9 * DIL_QBLK * DIL_KWIN * 4 + 2 * S * DIL_DH * 2
            + 3 * 3 * S * DIL_DH * 4 + (8 << 20))
    return pl.pallas_call(
        functools.partial(_dil_kernel, cfgs=cfgs, seq=S),
        grid=(B, G),
        in_specs=in_specs,
        out_specs=pl.BlockSpec((1, S, DIL_DH), lambda b, j: (b, 0, j)),
        out_shape=jax.ShapeDtypeStruct((B, S, G * DIL_DH), BF16),
        scratch_shapes=[pltpu.VMEM((3, S, DIL_DH), F32)] * 3,
        compiler_params=_cparams(vmem, 2),
        name="dilated_attention",
    )(*args)


def _merge_kernel(x_ref, yr_ref, yd_ref, gt_ref, wro_ref, wdo_ref, wout_ref, gn_ref, xo_ref, ho_ref):
    D = x_ref.shape[1]
    a = _dot(yr_ref[...], wro_ref[...])
    b = _dot(yd_ref[...], wdo_ref[...])
    mg = gt_ref[:, :D].astype(F32) * a + gt_ref[:, D:].astype(F32) * b
    xn = x_ref[...] + _dot(mg.astype(BF16), wout_ref[...])
    xo_ref[...] = xn
    ho_ref[...] = _rms(xn, gn_ref[...]).astype(ho_ref.dtype)


def _merge(x, yr, yd, gates, wro, wdo, wout, g_next):
    T, D = x.shape
    tm = MERGE_TILE
    row = lambda n: pl.BlockSpec((tm, n), lambda i: (i, 0))
    full = lambda a: pl.BlockSpec(a.shape, lambda i: (0, 0))
    vmem = ((wro.size + wdo.size + wout.size) * 2
            + 2 * tm * (D * 4 + yr.shape[1] * 2 + yd.shape[1] * 2 + gates.shape[1] * 2 + D * 4 + D * 2)
            + 6 * tm * D * 4)
    return pl.pallas_call(
        _merge_kernel,
        grid=(T // tm,),
        in_specs=[row(D), row(yr.shape[1]), row(yd.shape[1]), row(gates.shape[1]),
                  full(wro), full(wdo), full(wout), pl.BlockSpec((1, D), lambda i: (0, 0))],
        out_specs=[row(D), row(D)],
        out_shape=[jax.ShapeDtypeStruct((T, D), F32), jax.ShapeDtypeStruct((T, D), BF16)],
        compiler_params=_cparams(vmem, 1),
        name="merge",
    )(x, yr, yd, gates, wro, wdo, wout, g_next.reshape(1, D))


def _t5_bucket(rel):
    nb = N_BUCKETS // 2
    max_exact = nb // 2
    ret = jnp.where(rel > 0, nb, 0)
    n = jnp.abs(rel)
    nf = jnp.maximum(n, 1).astype(F32)
    large = max_exact + (jnp.log(nf / max_exact) / math.log(MAX_DISTANCE / max_exact)
                         * (nb - max_exact)).astype(jnp.int32)
    large = jnp.minimum(large, nb - 1)
    return ret + jnp.where(n < max_exact, n, large)


def _dil_bias_tiles(rel_bias):
    a = jnp.arange(DIL_QBLK)[:, None]
    c = jnp.arange(DIL_KWIN)[None, :]
    per_group = []
    for gi, (_, d) in enumerate(DIL_CONFIGS):
        tiles = []
        for delta in (0, DIL_R, 2 * DIL_R):
            off = c - delta - a
            valid = jnp.abs(off) <= DIL_R
            bucket = _t5_bucket(jnp.clip(off, -DIL_R, DIL_R) * d)
            rb = rel_bias[:, gi * DIL_GROUP_HEADS:(gi + 1) * DIL_GROUP_HEADS].astype(F32)
            hit = bucket[..., None] == jnp.arange(N_BUCKETS)
            b = jnp.sum(jnp.where(hit[..., None], rb[None, None], 0.0), axis=2)
            tiles.append(jnp.where(valid[..., None], b, NEG))
        per_group.append(jnp.stack(tiles, axis=0))
    t = jnp.stack(per_group, axis=0)
    return jnp.transpose(t, (4, 0, 1, 2, 3))


def _ret_decay_tables(decay_exp):
    C = RET_CHUNK
    lg = jnp.log1p(-jnp.exp2(-decay_exp.astype(F32)))
    lf = lg[0][:, None, None]
    lb = lg[1][:, None, None]
    idx = jnp.arange(C, dtype=F32)
    diff = idx[:, None] - idx[None, :]
    dm = jnp.where(diff[None] >= 0, jnp.exp(lf * jnp.maximum(diff, 0.0)[None]),
                   jnp.exp(lb * jnp.maximum(-diff, 0.0)[None]))
    col = lambda v: jnp.broadcast_to(v[:, :, None], (v.shape[0], C, C))
    qdf = col(jnp.exp(lg[0][:, None] * (idx + 1.0)[None]))
    qdb = col(jnp.exp(lg[1][:, None] * (C - idx)[None]))
    kdf = col(jnp.exp(lg[0][:, None] * (C - 1.0 - idx)[None]))
    kdb = col(jnp.exp(lg[1][:, None] * idx[None]))
    dec = jnp.stack([dm, qdf, qdb, kdf, kdb], axis=1)
    cdec = jnp.stack([jnp.exp(lg[0] * C), jnp.exp(lg[1] * C)], axis=1)
    return dec, cdec


def _rope_tables(S):
    half = RET_DK // 2
    pos = jnp.arange(S, dtype=F32)
    inv = ROPE_BASE ** (-jnp.arange(half, dtype=F32) / half)
    ang = pos[:, None] * inv[None, :]
    cos, sin = jnp.cos(ang), jnp.sin(ang)
    return jnp.concatenate([cos, cos], axis=-1), jnp.concatenate([-sin, sin], axis=-1)


def kernel(x, rel_bias, norm_ffn1, ffn1_gate, ffn1_up, ffn1_down, norm_mix, w_in, b_gate,
           ret_decay_exp, ret_norm, w_ret_o, dil_q_norm, dil_k_norm, w_dil_o, w_out, norm_ffn2,
           ffn2_gate, ffn2_up, ffn2_down):
    B, S, D = x.shape
    depth = w_in.shape[0]
    T = B * S
    assert S % TOKEN_TILE == 0 and T % MERGE_TILE == 0 and S % (16 * DIL_KWIN) == 0
    ret_qk = RET_HEADS * RET_DK
    ret_v = RET_HEADS * RET_DV
    dil_w = DIL_GROUP_HEADS * len(DIL_CONFIGS) * DIL_DH
    o_rq, o_rv = 0, 2 * ret_qk
    o_dq = 2 * ret_qk + 2 * ret_v
    o_dk, o_dv = o_dq + dil_w, o_dq + 2 * dil_w
    o_gate = o_dq + 3 * dil_w
    gw = DIL_GROUP_HEADS * DIL_DH

    cosf, sinf = _rope_tables(S)
    bias_tiles = _dil_bias_tiles(rel_bias)
    bf = lambda a: a.astype(BF16)

    xf = x.reshape(T, D)
    h = None
    for l in range(depth):
        xf, h = _ffn(xf, h, bf(ffn1_gate[l]), bf(ffn1_up[l]), bf(ffn1_down[l]),
                     g_in=norm_ffn1[l], g_next=norm_mix[l])

        w = w_in[l]
        wdil = [bf(jnp.concatenate([w[:, o + gi * gw:o + (gi + 1) * gw] for o in (o_dq, o_dk, o_dv)], axis=1))
                for gi in range(len(DIL_CONFIGS))]
        qk, vg, g0, g1, g2, gates = _projections(
            h, B, S, bf(w[:, o_rq:o_rv]), bf(w[:, o_rv:o_dq]), wdil, bf(w[:, o_gate:]), cosf, sinf,
            dil_q_norm[l].reshape(1, DIL_DH), dil_k_norm[l].reshape(1, DIL_DH), b_gate[l].reshape(1, 2 * D))

        dec, cdec = _ret_decay_tables(ret_decay_exp[l])
        yr = _retention(qk, vg, cdec, dec, ret_norm[l].reshape(RET_HEADS, 1, RET_DV))
        yd = _dilated_attention([g0, g1, g2], bias_tiles)

        xf, h = _merge(xf, yr.reshape(T, ret_v), yd.reshape(T, gw), gates,
                       bf(w_ret_o[l]), bf(w_dil_o[l]), bf(w_out[l]), norm_ffn2[l])
        if l + 1 < depth:
            xf, h = _ffn(xf, h, bf(ffn2_gate[l]), bf(ffn2_up[l]), bf(ffn2_down[l]), g_next=norm_ffn1[l + 1])
        else:
            xf, = _ffn(xf, h, bf(ffn2_gate[l]), bf(ffn2_up[l]), bf(ffn2_down[l]))
    return xf.reshape(B, S, D)
```

```python
import functools
import math

import jax
import jax.numpy as jnp
from jax import lax
from jax.experimental import pallas as pl
from jax.experimental.pallas import tpu as pltpu

F32 = jnp.float32
BF16 = jnp.bfloat16

V7X_LANES = 128
V7X_VMEM_BYTES = 64 * 1024 * 1024

EPS = 1e-6
ROPE_BASE = 10000.0
NEG = -1e30

RET_HEADS = 4
RET_DK = 128
RET_DV = 256
RET_CHUNK = 128
RET_BATCH = 8

DIL_CONFIGS = ((128, 1), (512, 4), (2048, 16))
DIL_GROUP_HEADS = 4
DIL_DH = 128
DIL_R = 64
DIL_QBLK = 2 * DIL_R
DIL_KWIN = 4 * DIL_R
DIL_BATCH = 8
DIL_MID = 4
DIL_STORE_DIL = (1, DIL_MID, DIL_MID)
N_BUCKETS = 32
MAX_DISTANCE = 1024

FFN_TILE = 1024
FFN_SUB = 512
TOKEN_TILE = 512
MERGE_TILE = 1024
FFN_CHUNK = 512
PROJ_CHUNK = 256


def _cparams(vmem_bytes, ngrid):
    return pltpu.CompilerParams(
        dimension_semantics=("arbitrary",) * ngrid,
        vmem_limit_bytes=int(min(vmem_bytes, V7X_VMEM_BYTES - (4 << 20))),
    )


def _dot(a, b):
    return jnp.dot(a, b, preferred_element_type=F32)


def _dot_nt(a, b):
    return lax.dot_general(a, b, (((1,), (1,)), ((), ())), preferred_element_type=F32)


def _dot_tn(a, b):
    return lax.dot_general(a, b, (((0,), (0,)), ((), ())), preferred_element_type=F32)


def _rms(x32, g):
    ms = jnp.mean(x32 * x32, axis=-1, keepdims=True)
    return x32 * lax.rsqrt(ms + EPS) * g


def _sigmoid(x):
    return 0.5 * jnp.tanh(0.5 * x) + 0.5


def _ffn_kernel(x_ref, hg_ref, wg_ref, wu_ref, wd_ref, *rest, chunks, norm_in):
    for r0 in range(0, x_ref.shape[0], FFN_SUB):
        rows = slice(r0, r0 + FFN_SUB)
        h = _rms(x_ref[rows, :], hg_ref[...]).astype(BF16) if norm_in else hg_ref[rows, :]
        acc = None
        for c0, cw in chunks:
            g = _dot(h, wg_ref[:, c0:c0 + cw])
            u = _dot(h, wu_ref[:, c0:c0 + cw])
            a = (g * _sigmoid(g) * u).astype(BF16)
            d = _dot(a, wd_ref[c0:c0 + cw, :])
            acc = d if acc is None else acc + d
        xn = x_ref[rows, :] + 0.5 * acc
        if len(rest) == 3:
            gn_ref, xo_ref, ho_ref = rest
            ho_ref[rows, :] = _rms(xn, gn_ref[...]).astype(ho_ref.dtype)
        else:
            xo_ref, = rest
        xo_ref[rows, :] = xn


def _ffn(x, h, wg, wu, wd, g_in=None, g_next=None):
    T, D = x.shape
    Fh = wg.shape[1]
    tm = FFN_TILE
    chunks = tuple((c0, min(FFN_CHUNK, Fh - c0)) for c0 in range(0, Fh, FFN_CHUNK))
    vmem = 3 * D * Fh * 2 + 2 * tm * D * (4 + 2 + 4 + 2) + 4 * tm * D * 4
    row = pl.BlockSpec((tm, D), lambda i: (i, 0))
    vec = pl.BlockSpec((1, D), lambda i: (0, 0))
    in_specs = [row, vec if h is None else row,
                pl.BlockSpec((D, Fh), lambda i: (0, 0)),
                pl.BlockSpec((D, Fh), lambda i: (0, 0)),
                pl.BlockSpec((Fh, D), lambda i: (0, 0))]
    args = [x, g_in.reshape(1, D) if h is None else h, wg, wu, wd]
    out_specs = [row]
    out_shape = [jax.ShapeDtypeStruct((T, D), F32)]
    if g_next is not None:
        in_specs.append(vec)
        args.append(g_next.reshape(1, D))
        out_specs.append(row)
        out_shape.append(jax.ShapeDtypeStruct((T, D), BF16))
    return pl.pallas_call(
        functools.partial(_ffn_kernel, chunks=chunks, norm_in=h is None),
        grid=(T // tm,),
        in_specs=in_specs,
        out_specs=out_specs,
        out_shape=out_shape,
        compiler_params=_cparams(vmem, 1),
        name="ffn",
    )(*args)


def _retqk_body(h_ref, w_ref, cos_ref, sin_ref, o_ref):
    cos = cos_ref[...]
    sin = sin_ref[...]
    per = PROJ_CHUNK // RET_DK
    for c in range(w_ref.shape[1] // PROJ_CHUNK):
        res = _dot(h_ref[...], w_ref[:, c * PROJ_CHUNK:(c + 1) * PROJ_CHUNK])
        for k in range(per):
            t = res[:, k * RET_DK:(k + 1) * RET_DK]
            r = t * cos + pltpu.roll(t, RET_DK // 2, axis=1) * sin
            hd = c * per + k
            if hd >= RET_HEADS:
                r = r * (RET_DK ** -0.5)
            o_ref[0, hd] = r.astype(o_ref.dtype)


def _retvg_body(h_ref, w_ref, o_ref):
    per = PROJ_CHUNK // RET_DV
    for c in range(w_ref.shape[1] // PROJ_CHUNK):
        res = _dot(h_ref[...], w_ref[:, c * PROJ_CHUNK:(c + 1) * PROJ_CHUNK])
        for k in range(per):
            t = res[:, k * RET_DV:(k + 1) * RET_DV]
            hd = c * per + k
            if hd >= RET_HEADS:
                t = t * _sigmoid(t)
            o_ref[0, hd] = t.astype(o_ref.dtype)


def _dil_body(h_ref, w_ref, qn, kn, o_ref, scr_ref, d):
    tm = h_ref.shape[0]
    per = PROJ_CHUNK // DIL_DH
    for c in range(w_ref.shape[1] // PROJ_CHUNK):
        res = _dot(h_ref[...], w_ref[:, c * PROJ_CHUNK:(c + 1) * PROJ_CHUNK])
        for k in range(per):
            t = res[:, k * DIL_DH:(k + 1) * DIL_DH]
            hd = c * per + k
            if hd < DIL_GROUP_HEADS:
                t = _rms(t, qn)
            elif hd < 2 * DIL_GROUP_HEADS:
                t = _rms(t, kn)
            if d == 1:
                o_ref[0, hd, 0] = t.astype(o_ref.dtype)
            else:
                scr_ref[hd] = t
    if d > 1:
        for hd in range(3 * DIL_GROUP_HEADS):
            for r in range(d):
                o_ref[0, hd, r] = scr_ref[hd, pl.ds(r, tm // d, stride=d), :].astype(o_ref.dtype)


def _gate_body(h_ref, w_ref, b_ref, o_ref):
    for c0 in range(0, w_ref.shape[1], PROJ_CHUNK):
        res = _dot(h_ref[...], w_ref[:, c0:c0 + PROJ_CHUNK]) + b_ref[:, c0:c0 + PROJ_CHUNK]
        o_ref[:, c0:c0 + PROJ_CHUNK] = _sigmoid(res).astype(o_ref.dtype)


def _proj_kernel(h_ref, wqk_ref, wvg_ref, wd0_ref, wd1_ref, wd2_ref, wgt_ref, cos_ref, sin_ref,
                 qn_ref, kn_ref, bg_ref, oqk_ref, ovg_ref, od0_ref, od1_ref, od2_ref, ogt_ref, scr_ref):
    _retqk_body(h_ref, wqk_ref, cos_ref, sin_ref, oqk_ref)
    _retvg_body(h_ref, wvg_ref, ovg_ref)
    qn = qn_ref[...] * (DIL_DH ** -0.5)
    kn = kn_ref[...]
    for w_ref, o_ref, d in zip((wd0_ref, wd1_ref, wd2_ref), (od0_ref, od1_ref, od2_ref), DIL_STORE_DIL):
        _dil_body(h_ref, w_ref, qn, kn, o_ref, scr_ref, d)
    _gate_body(h_ref, wgt_ref, bg_ref, ogt_ref)


def _projections(h, B, S, wqk, wvg, wdil, wgt, cosf, sinf, qn, kn, bg):
    T, D = h.shape
    tm = TOKEN_TILE
    nS = S // tm
    nh_dil = 3 * DIL_GROUP_HEADS
    full = lambda a: pl.BlockSpec(a.shape, lambda i: (0,) * a.ndim)
    seq_idx = lambda i: (i // nS, 0, i % nS, 0)
    in_specs = ([pl.BlockSpec((tm, D), lambda i: (i, 0)), full(wqk), full(wvg)] + [full(w) for w in wdil]
                + [full(wgt),
                   pl.BlockSpec((tm, RET_DK), lambda i: (i % nS, 0)),
                   pl.BlockSpec((tm, RET_DK), lambda i: (i % nS, 0)),
                   full(qn), full(kn), full(bg)])
    out_shape = [jax.ShapeDtypeStruct((B, 2 * RET_HEADS, S, RET_DK), BF16),
                 jax.ShapeDtypeStruct((B, 2 * RET_HEADS, S, RET_DV), BF16)]
    out_specs = [pl.BlockSpec((1, 2 * RET_HEADS, tm, RET_DK), seq_idx),
                 pl.BlockSpec((1, 2 * RET_HEADS, tm, RET_DV), seq_idx)]
    for d in DIL_STORE_DIL:
        out_shape.append(jax.ShapeDtypeStruct((B, nh_dil, d, S // d, DIL_DH), BF16))
        out_specs.append(pl.BlockSpec((1, nh_dil, d, tm // d, DIL_DH), lambda i: (i // nS, 0, 0, i % nS, 0)))
    out_shape.append(jax.ShapeDtypeStruct((T, wgt.shape[1]), BF16))
    out_specs.append(pl.BlockSpec((tm, wgt.shape[1]), lambda i: (i, 0)))
    n_all = wqk.shape[1] + wvg.shape[1] + sum(w.shape[1] for w in wdil) + wgt.shape[1]
    vmem = (D * n_all * 2 + 2 * tm * D * 2 + 2 * tm * n_all * 2 + nh_dil * tm * DIL_DH * 4
            + 8 * tm * PROJ_CHUNK * 4)
    return pl.pallas_call(
        _proj_kernel,
        grid=(T // tm,),
        in_specs=in_specs,
        out_specs=out_specs,
        out_shape=out_shape,
        scratch_shapes=[pltpu.VMEM((nh_dil, tm, DIL_DH), F32)],
        compiler_params=_cparams(vmem, 1),
        name="projections",
    )(h, wqk, wvg, *wdil, wgt, cosf, sinf, qn, kn, bg)


def _ret_kernel(cdec_ref, q_ref, k_ref, v_ref, sg_ref, dec_ref, ng_ref, o_ref,
                kvf_scr, kvb_scr, fst_scr, bst_scr, *, nchunks):
    C = RET_CHUNK
    hd = pl.program_id(1)
    cf = cdec_ref[hd, 0]
    cb = cdec_ref[hd, 1]

    def rows(n):
        return pl.ds(pl.multiple_of(n * C, C), C)

    def kv_body(n, carry):
        kn = k_ref[0, 0, rows(n), :].astype(F32)
        vn = v_ref[0, 0, rows(n), :]
        kvf_scr[n] = _dot_tn((kn * dec_ref[0, 3]).astype(BF16), vn)
        kvb_scr[n] = _dot_tn((kn * dec_ref[0, 4]).astype(BF16), vn)
        return carry

    lax.fori_loop(0, nchunks, kv_body, 0, unroll=16)

    def scan_f(n, st):
        fst_scr[n] = st.astype(BF16)
        return cf * st + kvf_scr[n]

    lax.fori_loop(0, nchunks, scan_f, jnp.zeros((RET_DK, RET_DV), F32))

    def scan_b(i, st):
        n = nchunks - 1 - i
        bst_scr[n] = st.astype(BF16)
        return cb * st + kvb_scr[n]

    lax.fori_loop(0, nchunks, scan_b, jnp.zeros((RET_DK, RET_DV), F32))

    G = RET_BATCH

    def out_body(t, carry):
        rw = pl.ds(pl.multiple_of(t * (G * C), G * C), G * C)
        cs = pl.ds(pl.multiple_of(t * G, G), G)
        q = q_ref[0, 0, rw, :].reshape(G, C, RET_DK)
        k = k_ref[0, 0, rw, :].reshape(G, C, RET_DK)
        v = v_ref[0, 0, rw, :].reshape(G, C, RET_DV)
        q32 = q.astype(F32)
        s = jnp.einsum('gcd,gjd->gcj', q, k, preferred_element_type=F32) * dec_ref[0, 0]
        lhs = jnp.concatenate([s.astype(BF16), (q32 * dec_ref[0, 1]).astype(BF16),
                               (q32 * dec_ref[0, 2]).astype(BF16)], axis=2)
        rhs = jnp.concatenate([v, fst_scr[cs], bst_scr[cs]], axis=1)
        o = jnp.einsum('gck,gke->gce', lhs, rhs, preferred_element_type=F32)
        r = lax.rsqrt(jnp.mean(o * o, axis=-1, keepdims=True) + EPS)
        y = (o * r * ng_ref[0]).reshape(G * C, RET_DV) * sg_ref[0, 0, rw, :].astype(F32)
        o_ref[0, rw, :] = y.astype(o_ref.dtype)
        return carry

    lax.fori_loop(0, nchunks // G, out_body, 0, unroll=2)


def _retention(qk, vg, cdec, dec, ng):
    B, _, S, _ = qk.shape
    H = RET_HEADS
    nchunks = S // RET_CHUNK
    vmem = (2 * (2 * S * RET_DK * 2 + 2 * S * RET_DV * 2 + S * RET_DV * 2)
            + nchunks * RET_DK * RET_DV * (4 + 4 + 2 + 2) + (8 << 20))
    return pl.pallas_call(
        functools.partial(_ret_kernel, nchunks=nchunks),
        grid=(B, H),
        in_specs=[pl.BlockSpec(memory_space=pltpu.SMEM),
                  pl.BlockSpec((1, 1, S, RET_DK), lambda b, h: (b, h, 0, 0)),
                  pl.BlockSpec((1, 1, S, RET_DK), lambda b, h: (b, H + h, 0, 0)),
                  pl.BlockSpec((1, 1, S, RET_DV), lambda b, h: (b, h, 0, 0)),
                  pl.BlockSpec((1, 1, S, RET_DV), lambda b, h: (b, H + h, 0, 0)),
                  pl.BlockSpec((1, 5, RET_CHUNK, RET_CHUNK), lambda b, h: (h, 0, 0, 0)),
                  pl.BlockSpec((1, 1, RET_DV), lambda b, h: (h, 0, 0))],
        out_specs=pl.BlockSpec((1, S, RET_DV), lambda b, h: (b, 0, h)),
        out_shape=jax.ShapeDtypeStruct((B, S, H * RET_DV), BF16),
        scratch_shapes=[pltpu.VMEM((nchunks, RET_DK, RET_DV), F32),
                        pltpu.VMEM((nchunks, RET_DK, RET_DV), F32),
                        pltpu.VMEM((nchunks, RET_DK, RET_DV), BF16),
                        pltpu.VMEM((nchunks, RET_DK, RET_DV), BF16)],
        compiler_params=_cparams(vmem, 2),
        name="retention",
    )(cdec, qk, qk, vg, vg, dec, ng)


def _dil_kernel(q0, k0, v0, q1, k1, v1, q2, k2, v2, bias_ref, o_ref, osc, lsc, msc, *, seq):
    QB, KW, DH, MID, G = DIL_QBLK, DIL_KWIN, DIL_DH, DIL_MID, DIL_BATCH
    SM = seq // MID
    dils = tuple(d for _, d in DIL_CONFIGS)
    ones = jnp.ones((KW, DH), BF16)

    def window(n, L):
        nblk = L // QB
        if isinstance(n, int):
            i0 = n * QB
            return i0, min(max(i0 - DIL_R, 0), L - KW), (0 if n == 0 else 2 if n == nblk - 1 else 1)
        i0 = pl.multiple_of(n * QB, QB)
        w0 = pl.multiple_of(jnp.clip(i0 - DIL_R, 0, L - KW), DIL_R)
        return i0, w0, jnp.where(n == 0, 0, jnp.where(n == nblk - 1, 2, 1))

    def attend(g, tiles):
        q = jnp.stack([t[0] for t in tiles])
        k = jnp.stack([t[1] for t in tiles])
        v = jnp.stack([jnp.concatenate([t[2], ones], axis=1) for t in tiles])
        s = jnp.einsum('gqd,gkd->gqk', q, k, preferred_element_type=F32)
        s = s + jnp.stack([bias_ref[0, g, t[3]] for t in tiles])
        m = jnp.max(s, axis=-1, keepdims=True)
        p = jnp.exp(s - m).astype(BF16)
        ol = jnp.einsum('gqk,gke->gqe', p, v, preferred_element_type=F32)
        for i, t in enumerate(tiles):
            osc[g, t[4], :] = ol[i, :, :DH]
            lsc[g, t[4], :] = ol[i, :, DH:]
            msc[g, t[4], :] = jnp.broadcast_to(m[i], (QB, DH))

    def g0_body(t, carry):
        tiles = []
        for i in range(G):
            i0, w0, var = window(t * G + i, seq)
            tiles.append((q0[0, 0, 0, pl.ds(i0, QB), :], k0[0, 0, 0, pl.ds(w0, KW), :],
                          v0[0, 0, 0, pl.ds(w0, KW), :], var, pl.ds(i0, QB)))
        attend(0, tiles)
        return carry

    lax.fori_loop(0, seq // (QB * G), g0_body, 0, unroll=2)

    per1 = SM // (QB * G)

    def g1_body(t, carry):
        r = t // per1
        tiles = []
        for i in range(G):
            i0, w0, var = window(i if per1 == 1 else (t % per1) * G + i, SM)
            tiles.append((q1[0, 0, r, pl.ds(i0, QB), :], k1[0, 0, r, pl.ds(w0, KW), :],
                          v1[0, 0, r, pl.ds(w0, KW), :], var,
                          pl.ds(pl.multiple_of(r * SM + i0, QB), QB)))
        attend(1, tiles)
        return carry

    lax.fori_loop(0, MID * per1, g1_body, 0, unroll=2)

    L2 = seq // dils[2]
    pend = []
    for a in range(dils[2] // MID):
        lanes = slice(a * DH, (a + 1) * DH)
        for r in range(MID):
            for n in range(L2 // QB):
                i0, w0, var = window(n, L2)
                pend.append((q2[0, 0, r, i0:i0 + QB, lanes], k2[0, 0, r, w0:w0 + KW, lanes],
                             v2[0, 0, r, w0:w0 + KW, lanes], var,
                             pl.ds(r * SM + MID * i0 + a, QB, stride=MID)))
                if len(pend) == G:
                    attend(2, pend)
                    pend = []
    assert not pend

    CB = 256
    for r in range(MID):
        def comb(t, carry, r=r):
            i4 = pl.multiple_of(t * CB, CB)
            rows = (pl.ds(r + MID * i4, CB, stride=MID), pl.ds(r * SM + i4, CB), pl.ds(r * SM + i4, CB))
            ms = [msc[g, rows[g], :] for g in range(3)]
            mx = jnp.maximum(jnp.maximum(ms[0], ms[1]), ms[2])
            es = [jnp.exp(m - mx) for m in ms]
            num = sum(es[g] * osc[g, rows[g], :] for g in range(3))
            den = sum(es[g] * lsc[g, rows[g], :] for g in range(3))
            o_ref[0, 0, pl.ds(i4, CB), r * DH:(r + 1) * DH] = (num / den).astype(o_ref.dtype)
            return carry

        lax.fori_loop(0, SM // CB, comb, 0, unroll=2)


def _dilated_attention(groups, bias):
    B = groups[0].shape[0]
    S = groups[0].shape[2] * groups[0].shape[3]
    H, DH, MID = DIL_GROUP_HEADS, DIL_DH, DIL_MID
    fold = DIL_CONFIGS[2][1] // MID
    assert [d for _, d in DIL_CONFIGS] == [1, MID, MID * MID] and S % (MID * MID * DIL_KWIN) == 0
    g2 = groups[2].reshape(B, 3 * H, MID, S // (MID * fold), fold * DH)
    in_specs = []
    args = []
    for a in (groups[0], groups[1], g2):
        for part in range(3):
            in_specs.append(pl.BlockSpec((1, 1) + a.shape[2:],
                                         lambda b, j, part=part: (b, part * H + j, 0, 0, 0)))
            args.append(a)
    in_specs.append(pl.BlockSpec((1, 3, 3, DIL_QBLK, DIL_KWIN), lambda b, j: (j, 0, 0, 0, 0)))
    args.append(bias)
    vmem = (2 * 9 * S * DH * 2 + 2 * 9 * DIL_QBLK * DIL_KWIN * 4 + 2 * S * DH * 2
            + 3 * 3 * S * DH * 4 + (8 << 20))
    out = pl.pallas_call(
        functools.partial(_dil_kernel, seq=S),
        grid=(B, H),
        in_specs=in_specs,
        out_specs=pl.BlockSpec((1, 1, S // MID, MID * DH), lambda b, j: (b, j, 0, 0)),
        out_shape=jax.ShapeDtypeStruct((B, H, S // MID, MID * DH), BF16),
        scratch_shapes=[pltpu.VMEM((3, S, DH), F32)] * 3,
        compiler_params=_cparams(vmem, 2),
        name="dilated_attention",
    )(*args)
    return out.reshape(B, H, S, DH)


def _merge_kernel(x_ref, yr_ref, yd_ref, gt_ref, wro_ref, wdo_ref, wout_ref, gn_ref, xo_ref, ho_ref):
    D = x_ref.shape[1]
    a = _dot(yr_ref[...], wro_ref[...])
    yd = jnp.concatenate([yd_ref[0, j] for j in range(yd_ref.shape[1])], axis=1)
    b = _dot(yd, wdo_ref[...])
    mg = gt_ref[:, :D].astype(F32) * a + gt_ref[:, D:].astype(F32) * b
    xn = x_ref[...] + _dot(mg.astype(BF16), wout_ref[...])
    xo_ref[...] = xn
    ho_ref[...] = _rms(xn, gn_ref[...]).astype(ho_ref.dtype)


def _merge(x, yr, yd, gates, wro, wdo, wout, g_next):
    T, D = x.shape
    tm = MERGE_TILE
    _, H, S, DH = yd.shape
    nS = S // tm
    row = lambda n: pl.BlockSpec((tm, n), lambda i: (i, 0))
    full = lambda a: pl.BlockSpec(a.shape, lambda i: (0, 0))
    vmem = ((wro.size + wdo.size + wout.size) * 2
            + 2 * tm * (D * 4 + yr.shape[1] * 2 + H * DH * 2 + gates.shape[1] * 2 + D * 4 + D * 2)
            + 6 * tm * D * 4)
    return pl.pallas_call(
        _merge_kernel,
        grid=(T // tm,),
        in_specs=[row(D), row(yr.shape[1]),
                  pl.BlockSpec((1, H, tm, DH), lambda i: (i // nS, 0, i % nS, 0)), row(gates.shape[1]),
                  full(wro), full(wdo), full(wout), pl.BlockSpec((1, D), lambda i: (0, 0))],
        out_specs=[row(D), row(D)],
        out_shape=[jax.ShapeDtypeStruct((T, D), F32), jax.ShapeDtypeStruct((T, D), BF16)],
        compiler_params=_cparams(vmem, 1),
        name="merge",
    )(x, yr, yd, gates, wro, wdo, wout, g_next.reshape(1, D))


def _t5_bucket(rel):
    nb = N_BUCKETS // 2
    max_exact = nb // 2
    ret = jnp.where(rel > 0, nb, 0)
    n = jnp.abs(rel)
    nf = jnp.maximum(n, 1).astype(F32)
    large = max_exact + (jnp.log(nf / max_exact) / math.log(MAX_DISTANCE / max_exact)
                         * (nb - max_exact)).astype(jnp.int32)
    large = jnp.minimum(large, nb - 1)
    return ret + jnp.where(n < max_exact, n, large)


def _dil_bias_tiles(rel_bias):
    a = jnp.arange(DIL_QBLK)[:, None]
    c = jnp.arange(DIL_KWIN)[None, :]
    per_group = []
    for gi, (_, d) in enumerate(DIL_CONFIGS):
        tiles = []
        for delta in (0, DIL_R, 2 * DIL_R):
            off = c - delta - a
            valid = jnp.abs(off) <= DIL_R
            bucket = _t5_bucket(jnp.clip(off, -DIL_R, DIL_R) * d)
            rb = rel_bias[:, gi * DIL_GROUP_HEADS:(gi + 1) * DIL_GROUP_HEADS].astype(F32)
            hit = bucket[..., None] == jnp.arange(N_BUCKETS)
            b = jnp.sum(jnp.where(hit[..., None], rb[None, None], 0.0), axis=2)
            tiles.append(jnp.where(valid[..., None], b, NEG))
        per_group.append(jnp.stack(tiles, axis=0))
    t = jnp.stack(per_group, axis=0)
    return jnp.transpose(t, (4, 0, 1, 2, 3))


def _ret_decay_tables(decay_exp):
    C = RET_CHUNK
    lg = jnp.log1p(-jnp.exp2(-decay_exp.astype(F32)))
    lf = lg[0][:, None, None]
    lb = lg[1][:, None, None]
    idx = jnp.arange(C, dtype=F32)
    diff = idx[:, None] - idx[None, :]
    dm = jnp.where(diff[None] >= 0, jnp.exp(lf * jnp.maximum(diff, 0.0)[None]),
                   jnp.exp(lb * jnp.maximum(-diff, 0.0)[None]))
    col = lambda v: jnp.broadcast_to(v[:, :, None], (v.shape[0], C, C))
    qdf = col(jnp.exp(lg[0][:, None] * (idx + 1.0)[None]))
    qdb = col(jnp.exp(lg[1][:, None] * (C - idx)[None]))
    kdf = col(jnp.exp(lg[0][:, None] * (C - 1.0 - idx)[None]))
    kdb = col(jnp.exp(lg[1][:, None] * idx[None]))
    dec = jnp.stack([dm, qdf, qdb, kdf, kdb], axis=1)
    cdec = jnp.stack([jnp.exp(lg[0] * C), jnp.exp(lg[1] * C)], axis=1)
    return dec, cdec


def _rope_tables(S):
    half = RET_DK // 2
    pos = jnp.arange(S, dtype=F32)
    inv = ROPE_BASE ** (-jnp.arange(half, dtype=F32) / half)
    ang = pos[:, None] * inv[None, :]
    cos, sin = jnp.cos(ang), jnp.sin(ang)
    return jnp.concatenate([cos, cos], axis=-1), jnp.concatenate([-sin, sin], axis=-1)


def kernel(x, rel_bias, norm_ffn1, ffn1_gate, ffn1_up, ffn1_down, norm_mix, w_in, b_gate,
           ret_decay_exp, ret_norm, w_ret_o, dil_q_norm, dil_k_norm, w_dil_o, w_out, norm_ffn2,
           ffn2_gate, ffn2_up, ffn2_down):
    B, S, D = x.shape
    depth = w_in.shape[0]
    T = B * S
    assert S % TOKEN_TILE == 0 and S % MERGE_TILE == 0
    ret_qk = RET_HEADS * RET_DK
    ret_v = RET_HEADS * RET_DV
    dil_w = DIL_GROUP_HEADS * len(DIL_CONFIGS) * DIL_DH
    o_rq, o_rv = 0, 2 * ret_qk
    o_dq = 2 * ret_qk + 2 * ret_v
    o_dk, o_dv = o_dq + dil_w, o_dq + 2 * dil_w
    o_gate = o_dq + 3 * dil_w
    gw = DIL_GROUP_HEADS * DIL_DH

    cosf, sinf = _rope_tables(S)
    bias_tiles = _dil_bias_tiles(rel_bias)
    bf = lambda a: a.astype(BF16)

    xf = x.reshape(T, D)
    h = None
    for l in range(depth):
        xf, h = _ffn(xf, h, bf(ffn1_gate[l]), bf(ffn1_up[l]), bf(ffn1_down[l]),
                     g_in=norm_ffn1[l], g_next=norm_mix[l])

        w = w_in[l]
        wdil = [bf(jnp.concatenate([w[:, o + gi * gw:o + (gi + 1) * gw] for o in (o_dq, o_dk, o_dv)], axis=1))
                for gi in range(len(DIL_CONFIGS))]
        qk, vg, g0, g1, g2, gates = _projections(
            h, B, S, bf(w[:, o_rq:o_rv]), bf(w[:, o_rv:o_dq]), wdil, bf(w[:, o_gate:]), cosf, sinf,
            dil_q_norm[l].reshape(1, DIL_DH), dil_k_norm[l].reshape(1, DIL_DH), b_gate[l].reshape(1, 2 * D))

        dec, cdec = _ret_decay_tables(ret_decay_exp[l])
        yr = _retention(qk, vg, cdec, dec, ret_norm[l].reshape(RET_HEADS, 1, RET_DV))
        yd = _dilated_attention([g0, g1, g2], bias_tiles)

        xf, h = _merge(xf, yr.reshape(T, ret_v), yd, gates,
                       bf(w_ret_o[l]), bf(w_dil_o[l]), bf(w_out[l]), norm_ffn2[l])
        if l + 1 < depth:
            xf, h = _ffn(xf, h, bf(ffn2_gate[l]), bf(ffn2_up[l]), bf(ffn2_down[l]), g_next=norm_ffn1[l + 1])
        else:
            xf, = _ffn(xf, h, bf(ffn2_gate[l]), bf(ffn2_up[l]), bf(ffn2_down[l]))
    return xf.reshape(B, S, D)
```

```python
import functools
import math

import jax
import jax.numpy as jnp
from jax import lax
from jax.experimental import pallas as pl
from jax.experimental.pallas import tpu as pltpu

F32 = jnp.float32
BF16 = jnp.bfloat16

V7X_LANES = 128
V7X_VMEM_BYTES = 64 * 1024 * 1024

EPS = 1e-6
ROPE_BASE = 10000.0
NEG = -1e30

RET_HEADS = 4
RET_DK = 128
RET_DV = 256
RET_CHUNK = 128
RET_BATCH = 8

DIL_CONFIGS = ((128, 1), (512, 4), (2048, 16))
DIL_GROUP_HEADS = 4
DIL_DH = 128
DIL_R = 64
DIL_QBLK = 2 * DIL_R
DIL_KWIN = 4 * DIL_R
DIL_BATCH = 8
DIL_MID = 4
N_BUCKETS = 32
MAX_DISTANCE = 1024

FFN_TILE = 1024
FFN_SUB = 512
TOKEN_TILE = 512
MERGE_TILE = 1024
FFN_CHUNK = 512
PROJ_CHUNK = 256


def _cparams(vmem_bytes, ngrid):
    return pltpu.CompilerParams(
        dimension_semantics=("arbitrary",) * ngrid,
        vmem_limit_bytes=int(min(vmem_bytes, V7X_VMEM_BYTES - (4 << 20))),
    )


def _dot(a, b):
    return jnp.dot(a, b, preferred_element_type=F32)


def _dot_nt(a, b):
    return lax.dot_general(a, b, (((1,), (1,)), ((), ())), preferred_element_type=F32)


def _dot_tn(a, b):
    return lax.dot_general(a, b, (((0,), (0,)), ((), ())), preferred_element_type=F32)


def _rms(x32, g):
    ms = jnp.mean(x32 * x32, axis=-1, keepdims=True)
    return x32 * lax.rsqrt(ms + EPS) * g


def _sigmoid(x):
    return 0.5 * jnp.tanh(0.5 * x) + 0.5


def _ffn_kernel(x_ref, hg_ref, wg_ref, wu_ref, wd_ref, *rest, chunks, norm_in):
    for r0 in range(0, x_ref.shape[0], FFN_SUB):
        rows = slice(r0, r0 + FFN_SUB)
        h = _rms(x_ref[rows, :], hg_ref[...]).astype(BF16) if norm_in else hg_ref[rows, :]
        acc = None
        for c0, cw in chunks:
            g = _dot(h, wg_ref[:, c0:c0 + cw])
            u = _dot(h, wu_ref[:, c0:c0 + cw])
            a = (g * _sigmoid(g) * u).astype(BF16)
            d = _dot(a, wd_ref[c0:c0 + cw, :])
            acc = d if acc is None else acc + d
        xn = x_ref[rows, :] + 0.5 * acc
        if len(rest) == 3:
            gn_ref, xo_ref, ho_ref = rest
            ho_ref[rows, :] = _rms(xn, gn_ref[...]).astype(ho_ref.dtype)
        else:
            xo_ref, = rest
        xo_ref[rows, :] = xn


def _ffn(x, h, wg, wu, wd, g_in=None, g_next=None):
    T, D = x.shape
    Fh = wg.shape[1]
    tm = FFN_TILE
    chunks = tuple((c0, min(FFN_CHUNK, Fh - c0)) for c0 in range(0, Fh, FFN_CHUNK))
    vmem = 3 * D * Fh * 2 + 2 * tm * D * (4 + 2 + 4 + 2) + 4 * tm * D * 4
    row = pl.BlockSpec((tm, D), lambda i: (i, 0))
    vec = pl.BlockSpec((1, D), lambda i: (0, 0))
    in_specs = [row, vec if h is None else row,
                pl.BlockSpec((D, Fh), lambda i: (0, 0)),
                pl.BlockSpec((D, Fh), lambda i: (0, 0)),
                pl.BlockSpec((Fh, D), lambda i: (0, 0))]
    args = [x, g_in.reshape(1, D) if h is None else h, wg, wu, wd]
    out_specs = [row]
    out_shape = [jax.ShapeDtypeStruct((T, D), F32)]
    if g_next is not None:
        in_specs.append(vec)
        args.append(g_next.reshape(1, D))
        out_specs.append(row)
        out_shape.append(jax.ShapeDtypeStruct((T, D), BF16))
    return pl.pallas_call(
        functools.partial(_ffn_kernel, chunks=chunks, norm_in=h is None),
        grid=(T // tm,),
        in_specs=in_specs,
        out_specs=out_specs,
        out_shape=out_shape,
        compiler_params=_cparams(vmem, 1),
        name="ffn",
    )(*args)


def _retqk_body(h_ref, w_ref, cos_ref, sin_ref, o_ref):
    cos = cos_ref[...]
    sin = sin_ref[...]
    per = PROJ_CHUNK // RET_DK
    for c in range(w_ref.shape[1] // PROJ_CHUNK):
        res = _dot(h_ref[...], w_ref[:, c * PROJ_CHUNK:(c + 1) * PROJ_CHUNK])
        for k in range(per):
            t = res[:, k * RET_DK:(k + 1) * RET_DK]
            r = t * cos + pltpu.roll(t, RET_DK // 2, axis=1) * sin
            hd = c * per + k
            if hd >= RET_HEADS:
                r = r * (RET_DK ** -0.5)
            o_ref[0, hd] = r.astype(o_ref.dtype)


def _retvg_body(h_ref, w_ref, o_ref):
    per = PROJ_CHUNK // RET_DV
    for c in range(w_ref.shape[1] // PROJ_CHUNK):
        res = _dot(h_ref[...], w_ref[:, c * PROJ_CHUNK:(c + 1) * PROJ_CHUNK])
        for k in range(per):
            t = res[:, k * RET_DV:(k + 1) * RET_DV]
            hd = c * per + k
            if hd >= RET_HEADS:
                t = t * _sigmoid(t)
            o_ref[0, hd] = t.astype(o_ref.dtype)


def _dil_body(h_ref, w_ref, qn, kn, o_ref, scr_ref, scr2_ref, d):
    tm = h_ref.shape[0]
    MID = DIL_MID
    per = PROJ_CHUNK // DIL_DH
    for c in range(w_ref.shape[1] // PROJ_CHUNK):
        res = _dot(h_ref[...], w_ref[:, c * PROJ_CHUNK:(c + 1) * PROJ_CHUNK])
        for k in range(per):
            t = res[:, k * DIL_DH:(k + 1) * DIL_DH]
            hd = c * per + k
            if hd < DIL_GROUP_HEADS:
                t = _rms(t, qn)
            elif hd < 2 * DIL_GROUP_HEADS:
                t = _rms(t, kn)
            if d == 1:
                o_ref[0, hd, 0] = t.astype(o_ref.dtype)
            else:
                scr_ref[hd] = t
    if d == MID:
        for hd in range(3 * DIL_GROUP_HEADS):
            for r in range(d):
                o_ref[0, hd, r] = scr_ref[hd, pl.ds(r, tm // d, stride=d), :].astype(o_ref.dtype)
    elif d == MID * MID:
        for hd in range(3 * DIL_GROUP_HEADS):
            for r in range(MID):
                scr2_ref[r] = scr_ref[hd, pl.ds(r, tm // MID, stride=MID), :]
            for r in range(MID):
                for a in range(MID):
                    o_ref[0, hd, MID * a + r] = (
                        scr2_ref[r, pl.ds(a, tm // d, stride=MID), :].astype(o_ref.dtype))
    else:
        assert d == 1


def _gate_body(h_ref, w_ref, b_ref, o_ref):
    for c0 in range(0, w_ref.shape[1], PROJ_CHUNK):
        res = _dot(h_ref[...], w_ref[:, c0:c0 + PROJ_CHUNK]) + b_ref[:, c0:c0 + PROJ_CHUNK]
        o_ref[:, c0:c0 + PROJ_CHUNK] = _sigmoid(res).astype(o_ref.dtype)


def _proj_kernel(h_ref, wqk_ref, wvg_ref, wd0_ref, wd1_ref, wd2_ref, wgt_ref, cos_ref, sin_ref,
                 qn_ref, kn_ref, bg_ref, oqk_ref, ovg_ref, od0_ref, od1_ref, od2_ref, ogt_ref,
                 scr_ref, scr2_ref):
    _retqk_body(h_ref, wqk_ref, cos_ref, sin_ref, oqk_ref)
    _retvg_body(h_ref, wvg_ref, ovg_ref)
    qn = qn_ref[...] * (DIL_DH ** -0.5)
    kn = kn_ref[...]
    for w_ref, o_ref, (_, d) in zip((wd0_ref, wd1_ref, wd2_ref), (od0_ref, od1_ref, od2_ref), DIL_CONFIGS):
        _dil_body(h_ref, w_ref, qn, kn, o_ref, scr_ref, scr2_ref, d)
    _gate_body(h_ref, wgt_ref, bg_ref, ogt_ref)


def _projections(h, B, S, wqk, wvg, wdil, wgt, cosf, sinf, qn, kn, bg):
    T, D = h.shape
    tm = TOKEN_TILE
    nS = S // tm
    nh_dil = 3 * DIL_GROUP_HEADS
    full = lambda a: pl.BlockSpec(a.shape, lambda i: (0,) * a.ndim)
    seq_idx = lambda i: (i // nS, 0, i % nS, 0)
    in_specs = ([pl.BlockSpec((tm, D), lambda i: (i, 0)), full(wqk), full(wvg)] + [full(w) for w in wdil]
                + [full(wgt),
                   pl.BlockSpec((tm, RET_DK), lambda i: (i % nS, 0)),
                   pl.BlockSpec((tm, RET_DK), lambda i: (i % nS, 0)),
                   full(qn), full(kn), full(bg)])
    out_shape = [jax.ShapeDtypeStruct((B, 2 * RET_HEADS, S, RET_DK), BF16),
                 jax.ShapeDtypeStruct((B, 2 * RET_HEADS, S, RET_DV), BF16)]
    out_specs = [pl.BlockSpec((1, 2 * RET_HEADS, tm, RET_DK), seq_idx),
                 pl.BlockSpec((1, 2 * RET_HEADS, tm, RET_DV), seq_idx)]
    for _, d in DIL_CONFIGS:
        out_shape.append(jax.ShapeDtypeStruct((B, nh_dil, d, S // d, DIL_DH), BF16))
        out_specs.append(pl.BlockSpec((1, nh_dil, d, tm // d, DIL_DH), lambda i: (i // nS, 0, 0, i % nS, 0)))
    out_shape.append(jax.ShapeDtypeStruct((T, wgt.shape[1]), BF16))
    out_specs.append(pl.BlockSpec((tm, wgt.shape[1]), lambda i: (i, 0)))
    n_all = wqk.shape[1] + wvg.shape[1] + sum(w.shape[1] for w in wdil) + wgt.shape[1]
    vmem = (D * n_all * 2 + 2 * tm * D * 2 + 2 * tm * n_all * 2 + nh_dil * tm * DIL_DH * 4
            + 8 * tm * PROJ_CHUNK * 4)
    return pl.pallas_call(
        _proj_kernel,
        grid=(T // tm,),
        in_specs=in_specs,
        out_specs=out_specs,
        out_shape=out_shape,
        scratch_shapes=[pltpu.VMEM((nh_dil, tm, DIL_DH), F32),
                        pltpu.VMEM((DIL_MID, tm // DIL_MID, DIL_DH), F32)],
        compiler_params=_cparams(vmem, 1),
        name="projections",
    )(h, wqk, wvg, *wdil, wgt, cosf, sinf, qn, kn, bg)


def _ret_kernel(cdec_ref, q_ref, k_ref, v_ref, sg_ref, dec_ref, ng_ref, o_ref,
                kvf_scr, kvb_scr, fst_scr, bst_scr, *, nchunks):
    C = RET_CHUNK
    hd = pl.program_id(1)
    cf = cdec_ref[hd, 0]
    cb = cdec_ref[hd, 1]

    def rows(n):
        return pl.ds(pl.multiple_of(n * C, C), C)

    def kv_body(n, carry):
        kn = k_ref[0, 0, rows(n), :].astype(F32)
        vn = v_ref[0, 0, rows(n), :]
        kvf_scr[n] = _dot_tn((kn * dec_ref[0, 3]).astype(BF16), vn)
        kvb_scr[n] = _dot_tn((kn * dec_ref[0, 4]).astype(BF16), vn)
        return carry

    lax.fori_loop(0, nchunks, kv_body, 0, unroll=16)

    def scan_f(n, st):
        fst_scr[n] = st.astype(BF16)
        return cf * st + kvf_scr[n]

    lax.fori_loop(0, nchunks, scan_f, jnp.zeros((RET_DK, RET_DV), F32))

    def scan_b(i, st):
        n = nchunks - 1 - i
        bst_scr[n] = st.astype(BF16)
        return cb * st + kvb_scr[n]

    lax.fori_loop(0, nchunks, scan_b, jnp.zeros((RET_DK, RET_DV), F32))

    G = RET_BATCH

    def out_body(t, carry):
        rw = pl.ds(pl.multiple_of(t * (G * C), G * C), G * C)
        cs = pl.ds(pl.multiple_of(t * G, G), G)
        q = q_ref[0, 0, rw, :].reshape(G, C, RET_DK)
        k = k_ref[0, 0, rw, :].reshape(G, C, RET_DK)
        v = v_ref[0, 0, rw, :].reshape(G, C, RET_DV)
        q32 = q.astype(F32)
        s = jnp.einsum('gcd,gjd->gcj', q, k, preferred_element_type=F32) * dec_ref[0, 0]
        lhs = jnp.concatenate([s.astype(BF16), (q32 * dec_ref[0, 1]).astype(BF16),
                               (q32 * dec_ref[0, 2]).astype(BF16)], axis=2)
        rhs = jnp.concatenate([v, fst_scr[cs], bst_scr[cs]], axis=1)
        o = jnp.einsum('gck,gke->gce', lhs, rhs, preferred_element_type=F32)
        r = lax.rsqrt(jnp.mean(o * o, axis=-1, keepdims=True) + EPS)
        y = (o * r * ng_ref[0]).reshape(G * C, RET_DV) * sg_ref[0, 0, rw, :].astype(F32)
        o_ref[0, rw, :] = y.astype(o_ref.dtype)
        return carry

    lax.fori_loop(0, nchunks // G, out_body, 0, unroll=2)


def _retention(qk, vg, cdec, dec, ng):
    B, _, S, _ = qk.shape
    H = RET_HEADS
    nchunks = S // RET_CHUNK
    vmem = (2 * (2 * S * RET_DK * 2 + 2 * S * RET_DV * 2 + S * RET_DV * 2)
            + nchunks * RET_DK * RET_DV * (4 + 4 + 2 + 2) + (8 << 20))
    return pl.pallas_call(
        functools.partial(_ret_kernel, nchunks=nchunks),
        grid=(B, H),
        in_specs=[pl.BlockSpec(memory_space=pltpu.SMEM),
                  pl.BlockSpec((1, 1, S, RET_DK), lambda b, h: (b, h, 0, 0)),
                  pl.BlockSpec((1, 1, S, RET_DK), lambda b, h: (b, H + h, 0, 0)),
                  pl.BlockSpec((1, 1, S, RET_DV), lambda b, h: (b, h, 0, 0)),
                  pl.BlockSpec((1, 1, S, RET_DV), lambda b, h: (b, H + h, 0, 0)),
                  pl.BlockSpec((1, 5, RET_CHUNK, RET_CHUNK), lambda b, h: (h, 0, 0, 0)),
                  pl.BlockSpec((1, 1, RET_DV), lambda b, h: (h, 0, 0))],
        out_specs=pl.BlockSpec((1, S, RET_DV), lambda b, h: (b, 0, h)),
        out_shape=jax.ShapeDtypeStruct((B, S, H * RET_DV), BF16),
        scratch_shapes=[pltpu.VMEM((nchunks, RET_DK, RET_DV), F32),
                        pltpu.VMEM((nchunks, RET_DK, RET_DV), F32),
                        pltpu.VMEM((nchunks, RET_DK, RET_DV), BF16),
                        pltpu.VMEM((nchunks, RET_DK, RET_DV), BF16)],
        compiler_params=_cparams(vmem, 2),
        name="retention",
    )(cdec, qk, qk, vg, vg, dec, ng)


def _dil_kernel(q0, k0, v0, q1, k1, v1, q2, k2, v2, bias_ref, o_ref, osc, lsc, msc, *, seq):
    QB, KW, DH, MID, G = DIL_QBLK, DIL_KWIN, DIL_DH, DIL_MID, DIL_BATCH
    SM = seq // MID
    dils = tuple(d for _, d in DIL_CONFIGS)
    ones = jnp.ones((KW, DH), BF16)

    def window(n, L):
        nblk = L // QB
        if isinstance(n, int):
            i0 = n * QB
            return i0, min(max(i0 - DIL_R, 0), L - KW), (0 if n == 0 else 2 if n == nblk - 1 else 1)
        i0 = pl.multiple_of(n * QB, QB)
        w0 = pl.multiple_of(jnp.clip(i0 - DIL_R, 0, L - KW), DIL_R)
        return i0, w0, jnp.where(n == 0, 0, jnp.where(n == nblk - 1, 2, 1))

    def attend(g, tiles):
        q = jnp.stack([t[0] for t in tiles])
        k = jnp.stack([t[1] for t in tiles])
        v = jnp.stack([jnp.concatenate([t[2], ones], axis=1) for t in tiles])
        s = jnp.einsum('gqd,gkd->gqk', q, k, preferred_element_type=F32)
        s = s + jnp.stack([bias_ref[0, g, t[3]] for t in tiles])
        m = jnp.max(s, axis=-1, keepdims=True)
        p = jnp.exp(s - m).astype(BF16)
        ol = jnp.einsum('gqk,gke->gqe', p, v, preferred_element_type=F32)
        for i, t in enumerate(tiles):
            osc[g, t[4], :] = ol[i, :, :DH]
            lsc[g, t[4], :] = ol[i, :, DH:]
            msc[g, t[4], :] = jnp.broadcast_to(m[i], (QB, DH))

    def g0_body(t, carry):
        tiles = []
        for i in range(G):
            i0, w0, var = window(t * G + i, seq)
            tiles.append((q0[0, 0, 0, pl.ds(i0, QB), :], k0[0, 0, 0, pl.ds(w0, KW), :],
                          v0[0, 0, 0, pl.ds(w0, KW), :], var, pl.ds(i0, QB)))
        attend(0, tiles)
        return carry

    lax.fori_loop(0, seq // (QB * G), g0_body, 0, unroll=2)

    per1 = SM // (QB * G)

    def g1_body(t, carry):
        r = t // per1
        tiles = []
        for i in range(G):
            i0, w0, var = window(i if per1 == 1 else (t % per1) * G + i, SM)
            tiles.append((q1[0, 0, r, pl.ds(i0, QB), :], k1[0, 0, r, pl.ds(w0, KW), :],
                          v1[0, 0, r, pl.ds(w0, KW), :], var,
                          pl.ds(pl.multiple_of(r * SM + i0, QB), QB)))
        attend(1, tiles)
        return carry

    lax.fori_loop(0, MID * per1, g1_body, 0, unroll=2)

    L2 = seq // dils[2]
    pend = []
    for a in range(dils[2] // MID):
        for r in range(MID):
            res = MID * a + r
            for n in range(L2 // QB):
                i0, w0, var = window(n, L2)
                pend.append((q2[0, 0, res, i0:i0 + QB, :], k2[0, 0, res, w0:w0 + KW, :],
                             v2[0, 0, res, w0:w0 + KW, :], var,
                             pl.ds(r * SM + MID * i0 + a, QB, stride=MID)))
                if len(pend) == G:
                    attend(2, pend)
                    pend = []
    assert not pend

    CB = 256
    for r in range(MID):
        def comb(t, carry, r=r):
            i4 = pl.multiple_of(t * CB, CB)
            rows = (pl.ds(r + MID * i4, CB, stride=MID), pl.ds(r * SM + i4, CB), pl.ds(r * SM + i4, CB))
            ms = [msc[g, rows[g], :] for g in range(3)]
            mx = jnp.maximum(jnp.maximum(ms[0], ms[1]), ms[2])
            es = [jnp.exp(m - mx) for m in ms]
            num = sum(es[g] * osc[g, rows[g], :] for g in range(3))
            den = sum(es[g] * lsc[g, rows[g], :] for g in range(3))
            o_ref[0, 0, pl.ds(i4, CB), r * DH:(r + 1) * DH] = (num / den).astype(o_ref.dtype)
            return carry

        lax.fori_loop(0, SM // CB, comb, 0, unroll=2)


def _dilated_attention(groups, bias):
    B = groups[0].shape[0]
    S = groups[0].shape[2] * groups[0].shape[3]
    H, DH, MID = DIL_GROUP_HEADS, DIL_DH, DIL_MID
    assert [d for _, d in DIL_CONFIGS] == [1, MID, MID * MID] and S % (MID * MID * DIL_KWIN) == 0
    in_specs = []
    args = []
    for a in groups:
        for part in range(3):
            in_specs.append(pl.BlockSpec((1, 1) + a.shape[2:],
                                         lambda b, j, part=part: (b, part * H + j, 0, 0, 0)))
            args.append(a)
    in_specs.append(pl.BlockSpec((1, 3, 3, DIL_QBLK, DIL_KWIN), lambda b, j: (j, 0, 0, 0, 0)))
    args.append(bias)
    vmem = (2 * 9 * S * DH * 2 + 2 * 9 * DIL_QBLK * DIL_KWIN * 4 + 2 * S * DH * 2
            + 3 * 3 * S * DH * 4 + (8 << 20))
    return pl.pallas_call(
        functools.partial(_dil_kernel, seq=S),
        grid=(B, H),
        in_specs=in_specs,
        out_specs=pl.BlockSpec((1, 1, S // MID, MID * DH), lambda b, j: (b, j, 0, 0)),
        out_shape=jax.ShapeDtypeStruct((B, H, S // MID, MID * DH), BF16),
        scratch_shapes=[pltpu.VMEM((3, S, DH), F32)] * 3,
        compiler_params=_cparams(vmem, 2),
        name="dilated_attention",
    )(*args)


def _merge_kernel(x_ref, yr_ref, yd_ref, gt_ref, wro_ref, wdo_ref, wout_ref, gn_ref, xo_ref, ho_ref, yd_scr):
    D = x_ref.shape[1]
    H, MID, DH = yd_scr.shape[0], DIL_MID, DIL_DH
    tm = x_ref.shape[0]
    a = _dot(yr_ref[...], wro_ref[...])
    for j in range(H):
        for r in range(MID):
            yd_scr[j, pl.ds(r, tm // MID, stride=MID), :] = yd_ref[0, j, :, r * DH:(r + 1) * DH].astype(F32)
    yd = jnp.concatenate([yd_scr[j].astype(BF16) for j in range(H)], axis=1)
    b = _dot(yd, wdo_ref[...])
    mg = gt_ref[:, :D].astype(F32) * a + gt_ref[:, D:].astype(F32) * b
    xn = x_ref[...] + _dot(mg.astype(BF16), wout_ref[...])
    xo_ref[...] = xn
    ho_ref[...] = _rms(xn, gn_ref[...]).astype(ho_ref.dtype)


def _merge(x, yr, yd, gates, wro, wdo, wout, g_next):
    T, D = x.shape
    tm = MERGE_TILE
    _, H, SM, W = yd.shape
    MID = DIL_MID
    nS = SM * MID // tm
    row = lambda n: pl.BlockSpec((tm, n), lambda i: (i, 0))
    full = lambda a: pl.BlockSpec(a.shape, lambda i: (0, 0))
    vmem = ((wro.size + wdo.size + wout.size) * 2
            + 2 * tm * (D * 4 + yr.shape[1] * 2 + H * DIL_DH * 2 + gates.shape[1] * 2 + D * 4 + D * 2)
            + H * tm * DIL_DH * 4 + 6 * tm * D * 4)
    return pl.pallas_call(
        _merge_kernel,
        grid=(T // tm,),
        in_specs=[row(D), row(yr.shape[1]),
                  pl.BlockSpec((1, H, tm // MID, W), lambda i: (i // nS, 0, i % nS, 0)), row(gates.shape[1]),
                  full(wro), full(wdo), full(wout), pl.BlockSpec((1, D), lambda i: (0, 0))],
        out_specs=[row(D), row(D)],
        out_shape=[jax.ShapeDtypeStruct((T, D), F32), jax.ShapeDtypeStruct((T, D), BF16)],
        scratch_shapes=[pltpu.VMEM((H, tm, DIL_DH), F32)],
        compiler_params=_cparams(vmem, 1),
        name="merge",
    )(x, yr, yd, gates, wro, wdo, wout, g_next.reshape(1, D))


def _t5_bucket(rel):
    nb = N_BUCKETS // 2
    max_exact = nb // 2
    ret = jnp.where(rel > 0, nb, 0)
    n = jnp.abs(rel)
    nf = jnp.maximum(n, 1).astype(F32)
    large = max_exact + (jnp.log(nf / max_exact) / math.log(MAX_DISTANCE / max_exact)
                         * (nb - max_exact)).astype(jnp.int32)
    large = jnp.minimum(large, nb - 1)
    return ret + jnp.where(n < max_exact, n, large)


def _dil_bias_tiles(rel_bias):
    a = jnp.arange(DIL_QBLK)[:, None]
    c = jnp.arange(DIL_KWIN)[None, :]
    per_group = []
    for gi, (_, d) in enumerate(DIL_CONFIGS):
        tiles = []
        for delta in (0, DIL_R, 2 * DIL_R):
            off = c - delta - a
            valid = jnp.abs(off) <= DIL_R
            bucket = _t5_bucket(jnp.clip(off, -DIL_R, DIL_R) * d)
            rb = rel_bias[:, gi * DIL_GROUP_HEADS:(gi + 1) * DIL_GROUP_HEADS].astype(F32)
            hit = bucket[..., None] == jnp.arange(N_BUCKETS)
            b = jnp.sum(jnp.where(hit[..., None], rb[None, None], 0.0), axis=2)
            tiles.append(jnp.where(valid[..., None], b, NEG))
        per_group.append(jnp.stack(tiles, axis=0))
    t = jnp.stack(per_group, axis=0)
    return jnp.transpose(t, (4, 0, 1, 2, 3))


def _ret_decay_tables(decay_exp):
    C = RET_CHUNK
    lg = jnp.log1p(-jnp.exp2(-decay_exp.astype(F32)))
    lf = lg[0][:, None, None]
    lb = lg[1][:, None, None]
    idx = jnp.arange(C, dtype=F32)
    diff = idx[:, None] - idx[None, :]
    dm = jnp.where(diff[None] >= 0, jnp.exp(lf * jnp.maximum(diff, 0.0)[None]),
                   jnp.exp(lb * jnp.maximum(-diff, 0.0)[None]))
    col = lambda v: jnp.broadcast_to(v[:, :, None], (v.shape[0], C, C))
    qdf = col(jnp.exp(lg[0][:, None] * (idx + 1.0)[None]))
    qdb = col(jnp.exp(lg[1][:, None] * (C - idx)[None]))
    kdf = col(jnp.exp(lg[0][:, None] * (C - 1.0 - idx)[None]))
    kdb = col(jnp.exp(lg[1][:, None] * idx[None]))
    dec = jnp.stack([dm, qdf, qdb, kdf, kdb], axis=1)
    cdec = jnp.stack([jnp.exp(lg[0] * C), jnp.exp(lg[1] * C)], axis=1)
    return dec, cdec


def _rope_tables(S):
    half = RET_DK // 2
    pos = jnp.arange(S, dtype=F32)
    inv = ROPE_BASE ** (-jnp.arange(half, dtype=F32) / half)
    ang = pos[:, None] * inv[None, :]
    cos, sin = jnp.cos(ang), jnp.sin(ang)
    return jnp.concatenate([cos, cos], axis=-1), jnp.concatenate([-sin, sin], axis=-1)


def kernel(x, rel_bias, norm_ffn1, ffn1_gate, ffn1_up, ffn1_down, norm_mix, w_in, b_gate,
           ret_decay_exp, ret_norm, w_ret_o, dil_q_norm, dil_k_norm, w_dil_o, w_out, norm_ffn2,
           ffn2_gate, ffn2_up, ffn2_down):
    B, S, D = x.shape
    depth = w_in.shape[0]
    T = B * S
    assert S % TOKEN_TILE == 0 and S % MERGE_TILE == 0
    ret_qk = RET_HEADS * RET_DK
    ret_v = RET_HEADS * RET_DV
    dil_w = DIL_GROUP_HEADS * len(DIL_CONFIGS) * DIL_DH
    o_rq, o_rv = 0, 2 * ret_qk
    o_dq = 2 * ret_qk + 2 * ret_v
    o_dk, o_dv = o_dq + dil_w, o_dq + 2 * dil_w
    o_gate = o_dq + 3 * dil_w
    gw = DIL_GROUP_HEADS * DIL_DH

    cosf, sinf = _rope_tables(S)
    bias_tiles = _dil_bias_tiles(rel_bias)
    bf = lambda a: a.astype(BF16)

    xf = x.reshape(T, D)
    h = None
    for l in range(depth):
        xf, h = _ffn(xf, h, bf(ffn1_gate[l]), bf(ffn1_up[l]), bf(ffn1_down[l]),
                     g_in=norm_ffn1[l], g_next=norm_mix[l])

        w = w_in[l]
        wdil = [bf(jnp.concatenate([w[:, o + gi * gw:o + (gi + 1) * gw] for o in (o_dq, o_dk, o_dv)], axis=1))
                for gi in range(len(DIL_CONFIGS))]
        qk, vg, g0, g1, g2, gates = _projections(
            h, B, S, bf(w[:, o_rq:o_rv]), bf(w[:, o_rv:o_dq]), wdil, bf(w[:, o_gate:]), cosf, sinf,
            dil_q_norm[l].reshape(1, DIL_DH), dil_k_norm[l].reshape(1, DIL_DH), b_gate[l].reshape(1, 2 * D))

        dec, cdec = _ret_decay_tables(ret_decay_exp[l])
        yr = _retention(qk, vg, cdec, dec, ret_norm[l].reshape(RET_HEADS, 1, RET_DV))
        yd = _dilated_attention([g0, g1, g2], bias_tiles)

        xf, h = _merge(xf, yr.reshape(T, ret_v), yd, gates,
                       bf(w_ret_o[l]), bf(w_dil_o[l]), bf(w_out[l]), norm_ffn2[l])
        if l + 1 < depth:
            xf, h = _ffn(xf, h, bf(ffn2_gate[l]), bf(ffn2_up[l]), bf(ffn2_down[l]), g_next=norm_ffn1[l + 1])
        else:
            xf, = _ffn(xf, h, bf(ffn2_gate[l]), bf(ffn2_up[l]), bf(ffn2_down[l]))
    return xf.reshape(B, S, D)
```

```python
import functools
import math

import jax
import jax.numpy as jnp
from jax import lax
from jax.experimental import pallas as pl
from jax.experimental.pallas import tpu as pltpu

F32 = jnp.float32
BF16 = jnp.bfloat16

V7X_LANES = 128
V7X_VMEM_BYTES = 64 * 1024 * 1024

EPS = 1e-6
ROPE_BASE = 10000.0
NEG = -1e30

RET_HEADS = 4
RET_DK = 128
RET_DV = 256
RET_CHUNK = 128
RET_BATCH = 8

DIL_CONFIGS = ((128, 1), (512, 4), (2048, 16))
DIL_GROUP_HEADS = 4
DIL_DH = 128
DIL_R = 64
DIL_QBLK = 2 * DIL_R
DIL_KWIN = 4 * DIL_R
DIL_BATCH = 8
DIL_MID = 4
N_BUCKETS = 32
MAX_DISTANCE = 1024

FFN_TILE = 1024
FFN_SUB = 512
TOKEN_TILE = 512
MERGE_TILE = 1024
MERGE_SUB = 512
FFN_CHUNK = 512
PROJ_CHUNK = 256


def _cparams(vmem_bytes, ngrid):
    return pltpu.CompilerParams(
        dimension_semantics=("arbitrary",) * ngrid,
        vmem_limit_bytes=int(min(vmem_bytes, V7X_VMEM_BYTES - (4 << 20))),
    )


def _dot(a, b):
    return jnp.dot(a, b, preferred_element_type=F32)


def _dot_nt(a, b):
    return lax.dot_general(a, b, (((1,), (1,)), ((), ())), preferred_element_type=F32)


def _dot_tn(a, b):
    return lax.dot_general(a, b, (((0,), (0,)), ((), ())), preferred_element_type=F32)


def _rms(x32, g):
    ms = jnp.mean(x32 * x32, axis=-1, keepdims=True)
    return x32 * lax.rsqrt(ms + EPS) * g


def _sigmoid(x):
    return 0.5 * jnp.tanh(0.5 * x) + 0.5


def _ffn_kernel(x_ref, hg_ref, wg_ref, wu_ref, wd_ref, *rest, chunks, norm_in):
    for r0 in range(0, x_ref.shape[0], FFN_SUB):
        rows = slice(r0, r0 + FFN_SUB)
        h = _rms(x_ref[rows, :], hg_ref[...]).astype(BF16) if norm_in else hg_ref[rows, :]
        acc = None
        for c0, cw in chunks:
            g = _dot(h, wg_ref[:, c0:c0 + cw])
            u = _dot(h, wu_ref[:, c0:c0 + cw])
            a = (g * _sigmoid(g) * u).astype(BF16)
            d = _dot(a, wd_ref[c0:c0 + cw, :])
            acc = d if acc is None else acc + d
        xn = x_ref[rows, :] + 0.5 * acc
        if len(rest) == 3:
            gn_ref, xo_ref, ho_ref = rest
            ho_ref[rows, :] = _rms(xn, gn_ref[...]).astype(ho_ref.dtype)
        else:
            xo_ref, = rest
        xo_ref[rows, :] = xn


def _ffn(x, h, wg, wu, wd, g_in=None, g_next=None):
    T, D = x.shape
    Fh = wg.shape[1]
    tm = FFN_TILE
    chunks = tuple((c0, min(FFN_CHUNK, Fh - c0)) for c0 in range(0, Fh, FFN_CHUNK))
    vmem = 3 * D * Fh * 2 + 2 * tm * D * (4 + 2 + 4 + 2) + 4 * tm * D * 4
    row = pl.BlockSpec((tm, D), lambda i: (i, 0))
    vec = pl.BlockSpec((1, D), lambda i: (0, 0))
    in_specs = [row, vec if h is None else row,
                pl.BlockSpec((D, Fh), lambda i: (0, 0)),
                pl.BlockSpec((D, Fh), lambda i: (0, 0)),
                pl.BlockSpec((Fh, D), lambda i: (0, 0))]
    args = [x, g_in.reshape(1, D) if h is None else h, wg, wu, wd]
    out_specs = [row]
    out_shape = [jax.ShapeDtypeStruct((T, D), F32)]
    if g_next is not None:
        in_specs.append(vec)
        args.append(g_next.reshape(1, D))
        out_specs.append(row)
        out_shape.append(jax.ShapeDtypeStruct((T, D), BF16))
    return pl.pallas_call(
        functools.partial(_ffn_kernel, chunks=chunks, norm_in=h is None),
        grid=(T // tm,),
        in_specs=in_specs,
        out_specs=out_specs,
        out_shape=out_shape,
        compiler_params=_cparams(vmem, 1),
        name="ffn",
    )(*args)


def _retqk_body(h_ref, w_ref, cos_ref, sin_ref, o_ref):
    cos = cos_ref[...]
    sin = sin_ref[...]
    per = PROJ_CHUNK // RET_DK
    for c in range(w_ref.shape[1] // PROJ_CHUNK):
        res = _dot(h_ref[...], w_ref[:, c * PROJ_CHUNK:(c + 1) * PROJ_CHUNK])
        for k in range(per):
            t = res[:, k * RET_DK:(k + 1) * RET_DK]
            r = t * cos + pltpu.roll(t, RET_DK // 2, axis=1) * sin
            hd = c * per + k
            if hd >= RET_HEADS:
                r = r * (RET_DK ** -0.5)
            o_ref[0, hd % RET_HEADS, hd // RET_HEADS] = r.astype(o_ref.dtype)


def _retvg_body(h_ref, w_ref, o_ref):
    per = PROJ_CHUNK // RET_DV
    for c in range(w_ref.shape[1] // PROJ_CHUNK):
        res = _dot(h_ref[...], w_ref[:, c * PROJ_CHUNK:(c + 1) * PROJ_CHUNK])
        for k in range(per):
            t = res[:, k * RET_DV:(k + 1) * RET_DV]
            hd = c * per + k
            if hd >= RET_HEADS:
                t = t * _sigmoid(t)
            o_ref[0, hd % RET_HEADS, hd // RET_HEADS] = t.astype(o_ref.dtype)


def _dil_body(h_ref, w_ref, qn, kn, o_ref, scr_ref, scr2_ref, d):
    tm = h_ref.shape[0]
    MID = DIL_MID
    per = PROJ_CHUNK // DIL_DH
    for c in range(w_ref.shape[1] // PROJ_CHUNK):
        res = _dot(h_ref[...], w_ref[:, c * PROJ_CHUNK:(c + 1) * PROJ_CHUNK])
        for k in range(per):
            t = res[:, k * DIL_DH:(k + 1) * DIL_DH]
            hd = c * per + k
            if hd < DIL_GROUP_HEADS:
                t = _rms(t, qn)
            elif hd < 2 * DIL_GROUP_HEADS:
                t = _rms(t, kn)
            if d == 1:
                o_ref[0, hd % DIL_GROUP_HEADS, hd // DIL_GROUP_HEADS, 0] = t.astype(o_ref.dtype)
            else:
                scr_ref[hd] = t
    if d == MID:
        for hd in range(3 * DIL_GROUP_HEADS):
            for r in range(d):
                o_ref[0, hd % DIL_GROUP_HEADS, hd // DIL_GROUP_HEADS, r] = (
                    scr_ref[hd, pl.ds(r, tm // d, stride=d), :].astype(o_ref.dtype))
    elif d == MID * MID:
        for hd in range(3 * DIL_GROUP_HEADS):
            for r in range(MID):
                scr2_ref[r] = scr_ref[hd, pl.ds(r, tm // MID, stride=MID), :]
            for r in range(MID):
                for a in range(MID):
                    o_ref[0, hd % DIL_GROUP_HEADS, hd // DIL_GROUP_HEADS, MID * a + r] = (
                        scr2_ref[r, pl.ds(a, tm // d, stride=MID), :].astype(o_ref.dtype))
    else:
        assert d == 1


def _gate_body(h_ref, w_ref, b_ref, o_ref):
    for c0 in range(0, w_ref.shape[1], PROJ_CHUNK):
        res = _dot(h_ref[...], w_ref[:, c0:c0 + PROJ_CHUNK]) + b_ref[:, c0:c0 + PROJ_CHUNK]
        o_ref[:, c0:c0 + PROJ_CHUNK] = _sigmoid(res).astype(o_ref.dtype)


def _proj_kernel(h_ref, wqk_ref, wvg_ref, wd0_ref, wd1_ref, wd2_ref, wgt_ref, cos_ref, sin_ref,
                 qn_ref, kn_ref, bg_ref, oqk_ref, ovg_ref, od0_ref, od1_ref, od2_ref, ogt_ref,
                 scr_ref, scr2_ref):
    _retqk_body(h_ref, wqk_ref, cos_ref, sin_ref, oqk_ref)
    _retvg_body(h_ref, wvg_ref, ovg_ref)
    qn = qn_ref[...] * (DIL_DH ** -0.5)
    kn = kn_ref[...]
    for w_ref, o_ref, (_, d) in zip((wd0_ref, wd1_ref, wd2_ref), (od0_ref, od1_ref, od2_ref), DIL_CONFIGS):
        _dil_body(h_ref, w_ref, qn, kn, o_ref, scr_ref, scr2_ref, d)
    _gate_body(h_ref, wgt_ref, bg_ref, ogt_ref)


def _projections(h, B, S, wqk, wvg, wdil, wgt, cosf, sinf, qn, kn, bg):
    T, D = h.shape
    tm = TOKEN_TILE
    nS = S // tm
    nh_dil = 3 * DIL_GROUP_HEADS
    full = lambda a: pl.BlockSpec(a.shape, lambda i: (0,) * a.ndim)
    seq_idx = lambda i: (i // nS, 0, 0, i % nS, 0)
    in_specs = ([pl.BlockSpec((tm, D), lambda i: (i, 0)), full(wqk), full(wvg)] + [full(w) for w in wdil]
                + [full(wgt),
                   pl.BlockSpec((tm, RET_DK), lambda i: (i % nS, 0)),
                   pl.BlockSpec((tm, RET_DK), lambda i: (i % nS, 0)),
                   full(qn), full(kn), full(bg)])
    out_shape = [jax.ShapeDtypeStruct((B, RET_HEADS, 2, S, RET_DK), BF16),
                 jax.ShapeDtypeStruct((B, RET_HEADS, 2, S, RET_DV), BF16)]
    out_specs = [pl.BlockSpec((1, RET_HEADS, 2, tm, RET_DK), seq_idx),
                 pl.BlockSpec((1, RET_HEADS, 2, tm, RET_DV), seq_idx)]
    for _, d in DIL_CONFIGS:
        out_shape.append(jax.ShapeDtypeStruct((B, DIL_GROUP_HEADS, 3, d, S // d, DIL_DH), BF16))
        out_specs.append(pl.BlockSpec((1, DIL_GROUP_HEADS, 3, d, tm // d, DIL_DH),
                                      lambda i: (i // nS, 0, 0, 0, i % nS, 0)))
    out_shape.append(jax.ShapeDtypeStruct((T, wgt.shape[1]), BF16))
    out_specs.append(pl.BlockSpec((tm, wgt.shape[1]), lambda i: (i, 0)))
    n_all = wqk.shape[1] + wvg.shape[1] + sum(w.shape[1] for w in wdil) + wgt.shape[1]
    vmem = (D * n_all * 2 + 2 * tm * D * 2 + 2 * tm * n_all * 2 + nh_dil * tm * DIL_DH * 4
            + 8 * tm * PROJ_CHUNK * 4)
    return pl.pallas_call(
        _proj_kernel,
        grid=(T // tm,),
        in_specs=in_specs,
        out_specs=out_specs,
        out_shape=out_shape,
        scratch_shapes=[pltpu.VMEM((nh_dil, tm, DIL_DH), F32),
                        pltpu.VMEM((DIL_MID, tm // DIL_MID, DIL_DH), F32)],
        compiler_params=_cparams(vmem, 1),
        name="projections",
    )(h, wqk, wvg, *wdil, wgt, cosf, sinf, qn, kn, bg)


def _ret_kernel(cdec_ref, qk_ref, vg_ref, dec_ref, ng_ref, o_ref,
                kvf_scr, kvb_scr, fst_scr, bst_scr, *, nchunks):
    C = RET_CHUNK
    q_ref, k_ref = qk_ref.at[0, 0, 0], qk_ref.at[0, 0, 1]
    v_ref, sg_ref = vg_ref.at[0, 0, 0], vg_ref.at[0, 0, 1]
    hd = pl.program_id(1)
    cf = cdec_ref[hd, 0]
    cb = cdec_ref[hd, 1]

    def rows(n):
        return pl.ds(pl.multiple_of(n * C, C), C)

    def kv_body(n, carry):
        kn = k_ref[rows(n), :].astype(F32)
        vn = v_ref[rows(n), :]
        kvf_scr[n] = _dot_tn((kn * dec_ref[0, 3]).astype(BF16), vn)
        kvb_scr[n] = _dot_tn((kn * dec_ref[0, 4]).astype(BF16), vn)
        return carry

    lax.fori_loop(0, nchunks, kv_body, 0, unroll=16)

    def scan_f(n, st):
        fst_scr[n] = st.astype(BF16)
        return cf * st + kvf_scr[n]

    lax.fori_loop(0, nchunks, scan_f, jnp.zeros((RET_DK, RET_DV), F32))

    def scan_b(i, st):
        n = nchunks - 1 - i
        bst_scr[n] = st.astype(BF16)
        return cb * st + kvb_scr[n]

    lax.fori_loop(0, nchunks, scan_b, jnp.zeros((RET_DK, RET_DV), F32))

    G = RET_BATCH

    def out_body(t, carry):
        rw = pl.ds(pl.multiple_of(t * (G * C), G * C), G * C)
        cs = pl.ds(pl.multiple_of(t * G, G), G)
        q = q_ref[rw, :].reshape(G, C, RET_DK)
        k = k_ref[rw, :].reshape(G, C, RET_DK)
        v = v_ref[rw, :].reshape(G, C, RET_DV)
        q32 = q.astype(F32)
        s = jnp.einsum('gcd,gjd->gcj', q, k, preferred_element_type=F32) * dec_ref[0, 0]
        lhs = jnp.concatenate([s.astype(BF16), (q32 * dec_ref[0, 1]).astype(BF16),
                               (q32 * dec_ref[0, 2]).astype(BF16)], axis=2)
        rhs = jnp.concatenate([v, fst_scr[cs], bst_scr[cs]], axis=1)
        o = jnp.einsum('gck,gke->gce', lhs, rhs, preferred_element_type=F32)
        r = lax.rsqrt(jnp.mean(o * o, axis=-1, keepdims=True) + EPS)
        y = (o * r * ng_ref[0]).reshape(G * C, RET_DV) * sg_ref[rw, :].astype(F32)
        o_ref[0, rw, :] = y.astype(o_ref.dtype)
        return carry

    lax.fori_loop(0, nchunks // G, out_body, 0, unroll=2)


def _retention(qk, vg, cdec, dec, ng):
    B, H, _, S, _ = qk.shape
    nchunks = S // RET_CHUNK
    vmem = (2 * (2 * S * RET_DK * 2 + 2 * S * RET_DV * 2 + S * RET_DV * 2)
            + nchunks * RET_DK * RET_DV * (4 + 4 + 2 + 2) + (8 << 20))
    return pl.pallas_call(
        functools.partial(_ret_kernel, nchunks=nchunks),
        grid=(B, H),
        in_specs=[pl.BlockSpec(memory_space=pltpu.SMEM),
                  pl.BlockSpec((1, 1, 2, S, RET_DK), lambda b, h: (b, h, 0, 0, 0)),
                  pl.BlockSpec((1, 1, 2, S, RET_DV), lambda b, h: (b, h, 0, 0, 0)),
                  pl.BlockSpec((1, 5, RET_CHUNK, RET_CHUNK), lambda b, h: (h, 0, 0, 0)),
                  pl.BlockSpec((1, 1, RET_DV), lambda b, h: (h, 0, 0))],
        out_specs=pl.BlockSpec((1, S, RET_DV), lambda b, h: (b, 0, h)),
        out_shape=jax.ShapeDtypeStruct((B, S, H * RET_DV), BF16),
        scratch_shapes=[pltpu.VMEM((nchunks, RET_DK, RET_DV), F32),
                        pltpu.VMEM((nchunks, RET_DK, RET_DV), F32),
                        pltpu.VMEM((nchunks, RET_DK, RET_DV), BF16),
                        pltpu.VMEM((nchunks, RET_DK, RET_DV), BF16)],
        compiler_params=_cparams(vmem, 2),
        name="retention",
    )(cdec, qk, vg, dec, ng)


def _dil_kernel(g0_ref, g1_ref, g2_ref, bias_ref, o_ref, osc, lsc, msc, *, seq):
    QB, KW, DH, MID, G = DIL_QBLK, DIL_KWIN, DIL_DH, DIL_MID, DIL_BATCH
    SM = seq // MID
    (q0, k0, v0), (q1, k1, v1), (q2, k2, v2) = (
        tuple(g.at[0, 0, part] for part in range(3)) for g in (g0_ref, g1_ref, g2_ref))
    dils = tuple(d for _, d in DIL_CONFIGS)
    ones = jnp.ones((KW, DH), BF16)

    def window(n, L):
        nblk = L // QB
        if isinstance(n, int):
            i0 = n * QB
            return i0, min(max(i0 - DIL_R, 0), L - KW), (0 if n == 0 else 2 if n == nblk - 1 else 1)
        i0 = pl.multiple_of(n * QB, QB)
        w0 = pl.multiple_of(jnp.clip(i0 - DIL_R, 0, L - KW), DIL_R)
        return i0, w0, jnp.where(n == 0, 0, jnp.where(n == nblk - 1, 2, 1))

    def attend(g, tiles):
        q = jnp.stack([t[0] for t in tiles])
        k = jnp.stack([t[1] for t in tiles])
        v = jnp.stack([jnp.concatenate([t[2], ones], axis=1) for t in tiles])
        s = jnp.einsum('gqd,gkd->gqk', q, k, preferred_element_type=F32)
        s = s + jnp.stack([bias_ref[0, g, t[3]] for t in tiles])
        m = jnp.max(s, axis=-1, keepdims=True)
        p = jnp.exp(s - m).astype(BF16)
        ol = jnp.einsum('gqk,gke->gqe', p, v, preferred_element_type=F32)
        for i, t in enumerate(tiles):
            osc[g, t[4], :] = ol[i, :, :DH]
            lsc[g, t[4], :] = ol[i, :, DH:]
            msc[g, t[4], :] = jnp.broadcast_to(m[i], (QB, DH))

    def g0_body(t, carry):
        tiles = []
        for i in range(G):
            i0, w0, var = window(t * G + i, seq)
            tiles.append((q0[0, pl.ds(i0, QB), :], k0[0, pl.ds(w0, KW), :],
                          v0[0, pl.ds(w0, KW), :], var, pl.ds(i0, QB)))
        attend(0, tiles)
        return carry

    lax.fori_loop(0, seq // (QB * G), g0_body, 0, unroll=2)

    per1 = SM // (QB * G)

    def g1_body(t, carry):
        r = t // per1
        tiles = []
        for i in range(G):
            i0, w0, var = window(i if per1 == 1 else (t % per1) * G + i, SM)
            tiles.append((q1[r, pl.ds(i0, QB), :], k1[r, pl.ds(w0, KW), :],
                          v1[r, pl.ds(w0, KW), :], var,
                          pl.ds(pl.multiple_of(r * SM + i0, QB), QB)))
        attend(1, tiles)
        return carry

    lax.fori_loop(0, MID * per1, g1_body, 0, unroll=2)

    L2 = seq // dils[2]
    pend = []
    for a in range(dils[2] // MID):
        for r in range(MID):
            res = MID * a + r
            for n in range(L2 // QB):
                i0, w0, var = window(n, L2)
                pend.append((q2[res, i0:i0 + QB, :], k2[res, w0:w0 + KW, :],
                             v2[res, w0:w0 + KW, :], var,
                             pl.ds(r * SM + MID * i0 + a, QB, stride=MID)))
                if len(pend) == G:
                    attend(2, pend)
                    pend = []
    assert not pend

    CB = 256
    for r in range(MID):
        def comb(t, carry, r=r):
            i4 = pl.multiple_of(t * CB, CB)
            rows = (pl.ds(r + MID * i4, CB, stride=MID), pl.ds(r * SM + i4, CB), pl.ds(r * SM + i4, CB))
            ms = [msc[g, rows[g], :] for g in range(3)]
            mx = jnp.maximum(jnp.maximum(ms[0], ms[1]), ms[2])
            es = [jnp.exp(m - mx) for m in ms]
            num = sum(es[g] * osc[g, rows[g], :] for g in range(3))
            den = sum(es[g] * lsc[g, rows[g], :] for g in range(3))
            o_ref[0, 0, pl.ds(i4, CB), r * DH:(r + 1) * DH] = (num / den).astype(o_ref.dtype)
            return carry

        lax.fori_loop(0, SM // CB, comb, 0, unroll=2)


def _dilated_attention(groups, bias):
    B = groups[0].shape[0]
    S = groups[0].shape[3] * groups[0].shape[4]
    H, DH, MID = DIL_GROUP_HEADS, DIL_DH, DIL_MID
    assert [d for _, d in DIL_CONFIGS] == [1, MID, MID * MID] and S % (MID * MID * DIL_KWIN) == 0
    in_specs = [pl.BlockSpec((1, 1) + a.shape[2:], lambda b, j: (b, j, 0, 0, 0, 0)) for a in groups]
    in_specs.append(pl.BlockSpec((1, 3, 3, DIL_QBLK, DIL_KWIN), lambda b, j: (j, 0, 0, 0, 0)))
    args = list(groups) + [bias]
    vmem = (2 * 9 * S * DH * 2 + 2 * 9 * DIL_QBLK * DIL_KWIN * 4 + 2 * S * DH * 2
            + 3 * 3 * S * DH * 4 + (8 << 20))
    return pl.pallas_call(
        functools.partial(_dil_kernel, seq=S),
        grid=(B, H),
        in_specs=in_specs,
        out_specs=pl.BlockSpec((1, 1, S // MID, MID * DH), lambda b, j: (b, j, 0, 0)),
        out_shape=jax.ShapeDtypeStruct((B, H, S // MID, MID * DH), BF16),
        scratch_shapes=[pltpu.VMEM((3, S, DH), F32)] * 3,
        compiler_params=_cparams(vmem, 2),
        name="dilated_attention",
    )(*args)


def _merge_kernel(x_ref, yr_ref, yd_ref, gt_ref, wro_ref, wdo_ref, wout_ref, gn_ref, xo_ref, ho_ref, yd_scr):
    D = x_ref.shape[1]
    H, MID, DH = yd_scr.shape[0], DIL_MID, DIL_DH
    for r0 in range(0, x_ref.shape[0], MERGE_SUB):
        rows = slice(r0, r0 + MERGE_SUB)
        a = _dot(yr_ref[rows, :], wro_ref[...])
        for j in range(H):
            for r in range(MID):
                yd_scr[j, pl.ds(r0 + r, MERGE_SUB // MID, stride=MID), :] = (
                    yd_ref[0, j, r0 // MID:(r0 + MERGE_SUB) // MID, r * DH:(r + 1) * DH].astype(F32))
        yd = jnp.concatenate([yd_scr[j, rows, :].astype(BF16) for j in range(H)], axis=1)
        b = _dot(yd, wdo_ref[...])
        mg = gt_ref[rows, :D].astype(F32) * a + gt_ref[rows, D:].astype(F32) * b
        xn = x_ref[rows, :] + _dot(mg.astype(BF16), wout_ref[...])
        xo_ref[rows, :] = xn
        ho_ref[rows, :] = _rms(xn, gn_ref[...]).astype(ho_ref.dtype)


def _merge(x, yr, yd, gates, wro, wdo, wout, g_next):
    T, D = x.shape
    tm = MERGE_TILE
    _, H, SM, W = yd.shape
    MID = DIL_MID
    nS = SM * MID // tm
    row = lambda n: pl.BlockSpec((tm, n), lambda i: (i, 0))
    full = lambda a: pl.BlockSpec(a.shape, lambda i: (0, 0))
    vmem = ((wro.size + wdo.size + wout.size) * 2
            + 2 * tm * (D * 4 + yr.shape[1] * 2 + H * DIL_DH * 2 + gates.shape[1] * 2 + D * 4 + D * 2)
            + H * tm * DIL_DH * 4 + 6 * tm * D * 4)
    return pl.pallas_call(
        _merge_kernel,
        grid=(T // tm,),
        in_specs=[row(D), row(yr.shape[1]),
                  pl.BlockSpec((1, H, tm // MID, W), lambda i: (i // nS, 0, i % nS, 0)), row(gates.shape[1]),
                  full(wro), full(wdo), full(wout), pl.BlockSpec((1, D), lambda i: (0, 0))],
        out_specs=[row(D), row(D)],
        out_shape=[jax.ShapeDtypeStruct((T, D), F32), jax.ShapeDtypeStruct((T, D), BF16)],
        scratch_shapes=[pltpu.VMEM((H, tm, DIL_DH), F32)],
        compiler_params=_cparams(vmem, 1),
        name="merge",
    )(x, yr, yd, gates, wro, wdo, wout, g_next.reshape(1, D))


def _t5_bucket(rel):
    nb = N_BUCKETS // 2
    max_exact = nb // 2
    ret = jnp.where(rel > 0, nb, 0)
    n = jnp.abs(rel)
    nf = jnp.maximum(n, 1).astype(F32)
    large = max_exact + (jnp.log(nf / max_exact) / math.log(MAX_DISTANCE / max_exact)
                         * (nb - max_exact)).astype(jnp.int32)
    large = jnp.minimum(large, nb - 1)
    return ret + jnp.where(n < max_exact, n, large)


def _dil_bias_tiles(rel_bias):
    a = jnp.arange(DIL_QBLK)[:, None]
    c = jnp.arange(DIL_KWIN)[None, :]
    per_group = []
    for gi, (_, d) in enumerate(DIL_CONFIGS):
        tiles = []
        for delta in (0, DIL_R, 2 * DIL_R):
            off = c - delta - a
            valid = jnp.abs(off) <= DIL_R
            bucket = _t5_bucket(jnp.clip(off, -DIL_R, DIL_R) * d)
            rb = rel_bias[:, gi * DIL_GROUP_HEADS:(gi + 1) * DIL_GROUP_HEADS].astype(F32)
            hit = bucket[..., None] == jnp.arange(N_BUCKETS)
            b = jnp.sum(jnp.where(hit[..., None], rb[None, None], 0.0), axis=2)
            tiles.append(jnp.where(valid[..., None], b, NEG))
        per_group.append(jnp.stack(tiles, axis=0))
    t = jnp.stack(per_group, axis=0)
    return jnp.transpose(t, (4, 0, 1, 2, 3))


def _ret_decay_tables(decay_exp):
    C = RET_CHUNK
    lg = jnp.log1p(-jnp.exp2(-decay_exp.astype(F32)))
    lf = lg[0][:, None, None]
    lb = lg[1][:, None, None]
    idx = jnp.arange(C, dtype=F32)
    diff = idx[:, None] - idx[None, :]
    dm = jnp.where(diff[None] >= 0, jnp.exp(lf * jnp.maximum(diff, 0.0)[None]),
                   jnp.exp(lb * jnp.maximum(-diff, 0.0)[None]))
    col = lambda v: jnp.broadcast_to(v[:, :, None], (v.shape[0], C, C))
    qdf = col(jnp.exp(lg[0][:, None] * (idx + 1.0)[None]))
    qdb = col(jnp.exp(lg[1][:, None] * (C - idx)[None]))
    kdf = col(jnp.exp(lg[0][:, None] * (C - 1.0 - idx)[None]))
    kdb = col(jnp.exp(lg[1][:, None] * idx[None]))
    dec = jnp.stack([dm, qdf, qdb, kdf, kdb], axis=1)
    cdec = jnp.stack([jnp.exp(lg[0] * C), jnp.exp(lg[1] * C)], axis=1)
    return dec, cdec


def _rope_tables(S):
    half = RET_DK // 2
    pos = jnp.arange(S, dtype=F32)
    inv = ROPE_BASE ** (-jnp.arange(half, dtype=F32) / half)
    ang = pos[:, None] * inv[None, :]
    cos, sin = jnp.cos(ang), jnp.sin(ang)
    return jnp.concatenate([cos, cos], axis=-1), jnp.concatenate([-sin, sin], axis=-1)


def kernel(x, rel_bias, norm_ffn1, ffn1_gate, ffn1_up, ffn1_down, norm_mix, w_in, b_gate,
           ret_decay_exp, ret_norm, w_ret_o, dil_q_norm, dil_k_norm, w_dil_o, w_out, norm_ffn2,
           ffn2_gate, ffn2_up, ffn2_down):
    B, S, D = x.shape
    depth = w_in.shape[0]
    T = B * S
    assert S % TOKEN_TILE == 0 and S % MERGE_TILE == 0
    ret_qk = RET_HEADS * RET_DK
    ret_v = RET_HEADS * RET_DV
    dil_w = DIL_GROUP_HEADS * len(DIL_CONFIGS) * DIL_DH
    o_rq, o_rv = 0, 2 * ret_qk
    o_dq = 2 * ret_qk + 2 * ret_v
    o_dk, o_dv = o_dq + dil_w, o_dq + 2 * dil_w
    o_gate = o_dq + 3 * dil_w
    gw = DIL_GROUP_HEADS * DIL_DH

    cosf, sinf = _rope_tables(S)
    bias_tiles = _dil_bias_tiles(rel_bias)
    bf = lambda a: a.astype(BF16)

    xf = x.reshape(T, D)
    h = None
    for l in range(depth):
        xf, h = _ffn(xf, h, bf(ffn1_gate[l]), bf(ffn1_up[l]), bf(ffn1_down[l]),
                     g_in=norm_ffn1[l], g_next=norm_mix[l])

        w = w_in[l]
        wdil = [bf(jnp.concatenate([w[:, o + gi * gw:o + (gi + 1) * gw] for o in (o_dq, o_dk, o_dv)], axis=1))
                for gi in range(len(DIL_CONFIGS))]
        qk, vg, g0, g1, g2, gates = _projections(
            h, B, S, bf(w[:, o_rq:o_rv]), bf(w[:, o_rv:o_dq]), wdil, bf(w[:, o_gate:]), cosf, sinf,
            dil_q_norm[l].reshape(1, DIL_DH), dil_k_norm[l].reshape(1, DIL_DH), b_gate[l].reshape(1, 2 * D))

        dec, cdec = _ret_decay_tables(ret_decay_exp[l])
        yr = _retention(qk, vg, cdec, dec, ret_norm[l].reshape(RET_HEADS, 1, RET_DV))
        yd = _dilated_attention([g0, g1, g2], bias_tiles)

        xf, h = _merge(xf, yr.reshape(T, ret_v), yd, gates,
                       bf(w_ret_o[l]), bf(w_dil_o[l]), bf(w_out[l]), norm_ffn2[l])
        if l + 1 < depth:
            xf, h = _ffn(xf, h, bf(ffn2_gate[l]), bf(ffn2_up[l]), bf(ffn2_down[l]), g_next=norm_ffn1[l + 1])
        else:
            xf, = _ffn(xf, h, bf(ffn2_gate[l]), bf(ffn2_up[l]), bf(ffn2_down[l]))
    return xf.reshape(B, S, D)
```

```python
import functools
import math

import jax
import jax.numpy as jnp
from jax import lax
from jax.experimental import pallas as pl
from jax.experimental.pallas import tpu as pltpu

F32 = jnp.float32
BF16 = jnp.bfloat16

V7X_LANES = 128
V7X_VMEM_BYTES = 64 * 1024 * 1024

EPS = 1e-6
ROPE_BASE = 10000.0
NEG = -1e30

RET_HEADS = 4
RET_DK = 128
RET_DV = 256
RET_CHUNK = 128
RET_BATCH = 8

DIL_CONFIGS = ((128, 1), (512, 4), (2048, 16))
DIL_GROUP_HEADS = 4
DIL_DH = 128
DIL_R = 64
DIL_QBLK = 2 * DIL_R
DIL_KWIN = 4 * DIL_R
DIL_BATCH = 8
DIL_MID = 4
N_BUCKETS = 32
MAX_DISTANCE = 1024

FFN_TILE = 1024
FFN_SUB = 512
TOKEN_TILE = 512
MERGE_TILE = 1024
MERGE_SUB = 512
FFN_CHUNK = 512
PROJ_CHUNK = 256


def _cparams(vmem_bytes, ngrid):
    return pltpu.CompilerParams(
        dimension_semantics=("arbitrary",) * ngrid,
        vmem_limit_bytes=int(min(vmem_bytes, V7X_VMEM_BYTES - (4 << 20))),
    )


def _layer_weight(w_all, layer):
    zeros = (0,) * (w_all.ndim - 1)
    return pl.BlockSpec((None,) + w_all.shape[1:], lambda *_: (layer,) + zeros,
                        pipeline_mode=pl.Buffered(1))


def _dot(a, b):
    return jnp.dot(a, b, preferred_element_type=F32)


def _dot_nt(a, b):
    return lax.dot_general(a, b, (((1,), (1,)), ((), ())), preferred_element_type=F32)


def _dot_tn(a, b):
    return lax.dot_general(a, b, (((0,), (0,)), ((), ())), preferred_element_type=F32)


def _rms(x32, g):
    ms = jnp.mean(x32 * x32, axis=-1, keepdims=True)
    return x32 * lax.rsqrt(ms + EPS) * g


def _sigmoid(x):
    return 0.5 * jnp.tanh(0.5 * x) + 0.5


def _ffn_kernel(x_ref, hg_ref, wg_ref, wu_ref, wd_ref, *rest, chunks, norm_in):
    for r0 in range(0, x_ref.shape[0], FFN_SUB):
        rows = slice(r0, r0 + FFN_SUB)
        h = _rms(x_ref[rows, :], hg_ref[...]).astype(BF16) if norm_in else hg_ref[rows, :]
        acc = None
        for c0, cw in chunks:
            g = _dot(h, wg_ref[:, c0:c0 + cw])
            u = _dot(h, wu_ref[:, c0:c0 + cw])
            a = (g * _sigmoid(g) * u).astype(BF16)
            d = _dot(a, wd_ref[c0:c0 + cw, :])
            acc = d if acc is None else acc + d
        xn = x_ref[rows, :] + 0.5 * acc
        if len(rest) == 3:
            gn_ref, xo_ref, ho_ref = rest
            ho_ref[rows, :] = _rms(xn, gn_ref[...]).astype(ho_ref.dtype)
        else:
            xo_ref, = rest
        xo_ref[rows, :] = xn


def _ffn(x, h, wg, wu, wd, layer, g_in=None, g_next=None):
    T, D = x.shape
    Fh = wg.shape[2]
    tm = FFN_TILE
    chunks = tuple((c0, min(FFN_CHUNK, Fh - c0)) for c0 in range(0, Fh, FFN_CHUNK))
    vmem = 3 * D * Fh * 2 + 2 * tm * D * (4 + 2 + 4 + 2) + 4 * tm * D * 4
    row = pl.BlockSpec((tm, D), lambda i: (i, 0))
    vec = pl.BlockSpec((1, D), lambda i: (0, 0))
    in_specs = [row, vec if h is None else row,
                _layer_weight(wg, layer), _layer_weight(wu, layer), _layer_weight(wd, layer)]
    args = [x, g_in.reshape(1, D) if h is None else h, wg, wu, wd]
    out_specs = [row]
    out_shape = [jax.ShapeDtypeStruct((T, D), F32)]
    if g_next is not None:
        in_specs.append(vec)
        args.append(g_next.reshape(1, D))
        out_specs.append(row)
        out_shape.append(jax.ShapeDtypeStruct((T, D), BF16))
    return pl.pallas_call(
        functools.partial(_ffn_kernel, chunks=chunks, norm_in=h is None),
        grid=(T // tm,),
        in_specs=in_specs,
        out_specs=out_specs,
        out_shape=out_shape,
        compiler_params=_cparams(vmem, 1),
        name="ffn",
    )(*args)


def _retqk_body(h_ref, w_ref, cos_ref, sin_ref, o_ref):
    cos = cos_ref[...]
    sin = sin_ref[...]
    per = PROJ_CHUNK // RET_DK
    for c in range(w_ref.shape[1] // PROJ_CHUNK):
        res = _dot(h_ref[...], w_ref[:, c * PROJ_CHUNK:(c + 1) * PROJ_CHUNK])
        for k in range(per):
            t = res[:, k * RET_DK:(k + 1) * RET_DK]
            r = t * cos + pltpu.roll(t, RET_DK // 2, axis=1) * sin
            hd = c * per + k
            if hd >= RET_HEADS:
                r = r * (RET_DK ** -0.5)
            o_ref[0, hd % RET_HEADS, hd // RET_HEADS] = r.astype(o_ref.dtype)


def _retvg_body(h_ref, w_ref, o_ref):
    per = PROJ_CHUNK // RET_DV
    for c in range(w_ref.shape[1] // PROJ_CHUNK):
        res = _dot(h_ref[...], w_ref[:, c * PROJ_CHUNK:(c + 1) * PROJ_CHUNK])
        for k in range(per):
            t = res[:, k * RET_DV:(k + 1) * RET_DV]
            hd = c * per + k
            if hd >= RET_HEADS:
                t = t * _sigmoid(t)
            o_ref[0, hd % RET_HEADS, hd // RET_HEADS] = t.astype(o_ref.dtype)


def _dil_body(h_ref, w_ref, qn, kn, o_ref, scr_ref, scr2_ref, d):
    tm = h_ref.shape[0]
    MID = DIL_MID
    per = PROJ_CHUNK // DIL_DH
    for c in range(w_ref.shape[1] // PROJ_CHUNK):
        res = _dot(h_ref[...], w_ref[:, c * PROJ_CHUNK:(c + 1) * PROJ_CHUNK])
        for k in range(per):
            t = res[:, k * DIL_DH:(k + 1) * DIL_DH]
            hd = c * per + k
            if hd < DIL_GROUP_HEADS:
                t = _rms(t, qn)
            elif hd < 2 * DIL_GROUP_HEADS:
                t = _rms(t, kn)
            if d == 1:
                o_ref[0, hd % DIL_GROUP_HEADS, hd // DIL_GROUP_HEADS, 0] = t.astype(o_ref.dtype)
            else:
                scr_ref[hd] = t
    if d == MID:
        for hd in range(3 * DIL_GROUP_HEADS):
            for r in range(d):
                o_ref[0, hd % DIL_GROUP_HEADS, hd // DIL_GROUP_HEADS, r] = (
                    scr_ref[hd, pl.ds(r, tm // d, stride=d), :].astype(o_ref.dtype))
    elif d == MID * MID:
        for hd in range(3 * DIL_GROUP_HEADS):
            for r in range(MID):
                scr2_ref[r] = scr_ref[hd, pl.ds(r, tm // MID, stride=MID), :]
            for r in range(MID):
                for a in range(MID):
                    o_ref[0, hd % DIL_GROUP_HEADS, hd // DIL_GROUP_HEADS, MID * a + r] = (
                        scr2_ref[r, pl.ds(a, tm // d, stride=MID), :].astype(o_ref.dtype))
    else:
        assert d == 1


def _gate_body(h_ref, w_ref, b_ref, o_ref):
    for c0 in range(0, w_ref.shape[1], PROJ_CHUNK):
        res = _dot(h_ref[...], w_ref[:, c0:c0 + PROJ_CHUNK]) + b_ref[:, c0:c0 + PROJ_CHUNK]
        o_ref[:, c0:c0 + PROJ_CHUNK] = _sigmoid(res).astype(o_ref.dtype)


def _proj_kernel(h_ref, wqk_ref, wvg_ref, wd0_ref, wd1_ref, wd2_ref, wgt_ref, cos_ref, sin_ref,
                 qn_ref, kn_ref, bg_ref, oqk_ref, ovg_ref, od0_ref, od1_ref, od2_ref, ogt_ref,
                 scr_ref, scr2_ref):
    _retqk_body(h_ref, wqk_ref, cos_ref, sin_ref, oqk_ref)
    _retvg_body(h_ref, wvg_ref, ovg_ref)
    qn = qn_ref[...] * (DIL_DH ** -0.5)
    kn = kn_ref[...]
    for w_ref, o_ref, (_, d) in zip((wd0_ref, wd1_ref, wd2_ref), (od0_ref, od1_ref, od2_ref), DIL_CONFIGS):
        _dil_body(h_ref, w_ref, qn, kn, o_ref, scr_ref, scr2_ref, d)
    _gate_body(h_ref, wgt_ref, bg_ref, ogt_ref)


def _projections(h, B, S, wqk, wvg, wdil, wgt, layer, cosf, sinf, qn, kn, bg):
    T, D = h.shape
    tm = TOKEN_TILE
    nS = S // tm
    nh_dil = 3 * DIL_GROUP_HEADS
    full = lambda a: pl.BlockSpec(a.shape, lambda i: (0,) * a.ndim)
    seq_idx = lambda i: (i // nS, 0, 0, i % nS, 0)
    lw = lambda w: _layer_weight(w, layer)
    in_specs = ([pl.BlockSpec((tm, D), lambda i: (i, 0)), lw(wqk), lw(wvg)] + [lw(w) for w in wdil]
                + [lw(wgt),
                   pl.BlockSpec((tm, RET_DK), lambda i: (i % nS, 0)),
                   pl.BlockSpec((tm, RET_DK), lambda i: (i % nS, 0)),
                   full(qn), full(kn), full(bg)])
    out_shape = [jax.ShapeDtypeStruct((B, RET_HEADS, 2, S, RET_DK), BF16),
                 jax.ShapeDtypeStruct((B, RET_HEADS, 2, S, RET_DV), BF16)]
    out_specs = [pl.BlockSpec((1, RET_HEADS, 2, tm, RET_DK), seq_idx),
                 pl.BlockSpec((1, RET_HEADS, 2, tm, RET_DV), seq_idx)]
    for _, d in DIL_CONFIGS:
        out_shape.append(jax.ShapeDtypeStruct((B, DIL_GROUP_HEADS, 3, d, S // d, DIL_DH), BF16))
        out_specs.append(pl.BlockSpec((1, DIL_GROUP_HEADS, 3, d, tm // d, DIL_DH),
                                      lambda i: (i // nS, 0, 0, 0, i % nS, 0)))
    out_shape.append(jax.ShapeDtypeStruct((T, wgt.shape[2]), BF16))
    out_specs.append(pl.BlockSpec((tm, wgt.shape[2]), lambda i: (i, 0)))
    n_all = wqk.shape[2] + wvg.shape[2] + sum(w.shape[2] for w in wdil) + wgt.shape[2]
    vmem = (D * n_all * 2 + 2 * tm * D * 2 + 2 * tm * n_all * 2 + nh_dil * tm * DIL_DH * 4
            + 8 * tm * PROJ_CHUNK * 4)
    return pl.pallas_call(
        _proj_kernel,
        grid=(T // tm,),
        in_specs=in_specs,
        out_specs=out_specs,
        out_shape=out_shape,
        scratch_shapes=[pltpu.VMEM((nh_dil, tm, DIL_DH), F32),
                        pltpu.VMEM((DIL_MID, tm // DIL_MID, DIL_DH), F32)],
        compiler_params=_cparams(vmem, 1),
        name="projections",
    )(h, wqk, wvg, *wdil, wgt, cosf, sinf, qn, kn, bg)


def _ret_kernel(cdec_ref, qk_ref, vg_ref, dec_ref, ng_ref, o_ref,
                kvf_scr, kvb_scr, fst_scr, bst_scr, *, nchunks):
    C = RET_CHUNK
    q_ref, k_ref = qk_ref.at[0, 0, 0], qk_ref.at[0, 0, 1]
    v_ref, sg_ref = vg_ref.at[0, 0, 0], vg_ref.at[0, 0, 1]
    hd = pl.program_id(1)
    cf = cdec_ref[hd, 0]
    cb = cdec_ref[hd, 1]

    def rows(n):
        return pl.ds(pl.multiple_of(n * C, C), C)

    def kv_body(n, carry):
        kn = k_ref[rows(n), :].astype(F32)
        vn = v_ref[rows(n), :]
        kvf_scr[n] = _dot_tn((kn * dec_ref[0, 3]).astype(BF16), vn)
        kvb_scr[n] = _dot_tn((kn * dec_ref[0, 4]).astype(BF16), vn)
        return carry

    lax.fori_loop(0, nchunks, kv_body, 0, unroll=16)

    def scan_f(n, st):
        fst_scr[n] = st.astype(BF16)
        return cf * st + kvf_scr[n]

    lax.fori_loop(0, nchunks, scan_f, jnp.zeros((RET_DK, RET_DV), F32))

    def scan_b(i, st):
        n = nchunks - 1 - i
        bst_scr[n] = st.astype(BF16)
        return cb * st + kvb_scr[n]

    lax.fori_loop(0, nchunks, scan_b, jnp.zeros((RET_DK, RET_DV), F32))

    G = RET_BATCH

    def out_body(t, carry):
        rw = pl.ds(pl.multiple_of(t * (G * C), G * C), G * C)
        cs = pl.ds(pl.multiple_of(t * G, G), G)
        q = q_ref[rw, :].reshape(G, C, RET_DK)
        k = k_ref[rw, :].reshape(G, C, RET_DK)
        v = v_ref[rw, :].reshape(G, C, RET_DV)
        q32 = q.astype(F32)
        s = jnp.einsum('gcd,gjd->gcj', q, k, preferred_element_type=F32) * dec_ref[0, 0]
        lhs = jnp.concatenate([s.astype(BF16), (q32 * dec_ref[0, 1]).astype(BF16),
                               (q32 * dec_ref[0, 2]).astype(BF16)], axis=2)
        rhs = jnp.concatenate([v, fst_scr[cs], bst_scr[cs]], axis=1)
        o = jnp.einsum('gck,gke->gce', lhs, rhs, preferred_element_type=F32)
        r = lax.rsqrt(jnp.mean(o * o, axis=-1, keepdims=True) + EPS)
        y = (o * r * ng_ref[0]).reshape(G * C, RET_DV) * sg_ref[rw, :].astype(F32)
        o_ref[0, rw, :] = y.astype(o_ref.dtype)
        return carry

    lax.fori_loop(0, nchunks // G, out_body, 0, unroll=2)


def _retention(qk, vg, cdec, dec, ng):
    B, H, _, S, _ = qk.shape
    nchunks = S // RET_CHUNK
    vmem = (2 * (2 * S * RET_DK * 2 + 2 * S * RET_DV * 2 + S * RET_DV * 2)
            + nchunks * RET_DK * RET_DV * (4 + 4 + 2 + 2) + (8 << 20))
    return pl.pallas_call(
        functools.partial(_ret_kernel, nchunks=nchunks),
        grid=(B, H),
        in_specs=[pl.BlockSpec(memory_space=pltpu.SMEM),
                  pl.BlockSpec((1, 1, 2, S, RET_DK), lambda b, h: (b, h, 0, 0, 0)),
                  pl.BlockSpec((1, 1, 2, S, RET_DV), lambda b, h: (b, h, 0, 0, 0)),
                  pl.BlockSpec((1, 5, RET_CHUNK, RET_CHUNK), lambda b, h: (h, 0, 0, 0)),
                  pl.BlockSpec((1, 1, RET_DV), lambda b, h: (h, 0, 0))],
        out_specs=pl.BlockSpec((1, S, RET_DV), lambda b, h: (b, 0, h)),
        out_shape=jax.ShapeDtypeStruct((B, S, H * RET_DV), BF16),
        scratch_shapes=[pltpu.VMEM((nchunks, RET_DK, RET_DV), F32),
                        pltpu.VMEM((nchunks, RET_DK, RET_DV), F32),
                        pltpu.VMEM((nchunks, RET_DK, RET_DV), BF16),
                        pltpu.VMEM((nchunks, RET_DK, RET_DV), BF16)],
        compiler_params=_cparams(vmem, 2),
        name="retention",
    )(cdec, qk, vg, dec, ng)


def _dil_kernel(g0_ref, g1_ref, g2_ref, bias_ref, o_ref, osc, lsc, msc, *, seq):
    QB, KW, DH, MID, G = DIL_QBLK, DIL_KWIN, DIL_DH, DIL_MID, DIL_BATCH
    SM = seq // MID
    (q0, k0, v0), (q1, k1, v1), (q2, k2, v2) = (
        tuple(g.at[0, 0, part] for part in range(3)) for g in (g0_ref, g1_ref, g2_ref))
    dils = tuple(d for _, d in DIL_CONFIGS)
    ones = jnp.ones((KW, DH), BF16)

    def window(n, L):
        nblk = L // QB
        if isinstance(n, int):
            i0 = n * QB
            return i0, min(max(i0 - DIL_R, 0), L - KW), (0 if n == 0 else 2 if n == nblk - 1 else 1)
        i0 = pl.multiple_of(n * QB, QB)
        w0 = pl.multiple_of(jnp.clip(i0 - DIL_R, 0, L - KW), DIL_R)
        return i0, w0, jnp.where(n == 0, 0, jnp.where(n == nblk - 1, 2, 1))

    def attend(g, tiles):
        q = jnp.stack([t[0] for t in tiles])
        k = jnp.stack([t[1] for t in tiles])
        v = jnp.stack([jnp.concatenate([t[2], ones], axis=1) for t in tiles])
        s = jnp.einsum('gqd,gkd->gqk', q, k, preferred_element_type=F32)
        s = s + jnp.stack([bias_ref[0, g, t[3]] for t in tiles])
        m = jnp.max(s, axis=-1, keepdims=True)
        p = jnp.exp(s - m).astype(BF16)
        ol = jnp.einsum('gqk,gke->gqe', p, v, preferred_element_type=F32)
        for i, t in enumerate(tiles):
            osc[g, t[4], :] = ol[i, :, :DH]
            lsc[g, t[4], :] = ol[i, :, DH:]
            msc[g, t[4], :] = jnp.broadcast_to(m[i], (QB, DH))

    def g0_body(t, carry):
        tiles = []
        for i in range(G):
            i0, w0, var = window(t * G + i, seq)
            tiles.append((q0[0, pl.ds(i0, QB), :], k0[0, pl.ds(w0, KW), :],
                          v0[0, pl.ds(w0, KW), :], var, pl.ds(i0, QB)))
        attend(0, tiles)
        return carry

    lax.fori_loop(0, seq // (QB * G), g0_body, 0, unroll=2)

    per1 = SM // (QB * G)

    def g1_body(t, carry):
        r = t // per1
        tiles = []
        for i in range(G):
            i0, w0, var = window(i if per1 == 1 else (t % per1) * G + i, SM)
            tiles.append((q1[r, pl.ds(i0, QB), :], k1[r, pl.ds(w0, KW), :],
                          v1[r, pl.ds(w0, KW), :], var,
                          pl.ds(pl.multiple_of(r * SM + i0, QB), QB)))
        attend(1, tiles)
        return carry

    lax.fori_loop(0, MID * per1, g1_body, 0, unroll=2)

    L2 = seq // dils[2]
    pend = []
    for a in range(dils[2] // MID):
        for r in range(MID):
            res = MID * a + r
            for n in range(L2 // QB):
                i0, w0, var = window(n, L2)
                pend.append((q2[res, i0:i0 + QB, :], k2[res, w0:w0 + KW, :],
                             v2[res, w0:w0 + KW, :], var,
                             pl.ds(r * SM + MID * i0 + a, QB, stride=MID)))
                if len(pend) == G:
                    attend(2, pend)
                    pend = []
    assert not pend

    CB = 256
    for r in range(MID):
        def comb(t, carry, r=r):
            i4 = pl.multiple_of(t * CB, CB)
            rows = (pl.ds(r + MID * i4, CB, stride=MID), pl.ds(r * SM + i4, CB), pl.ds(r * SM + i4, CB))
            ms = [msc[g, rows[g], :] for g in range(3)]
            mx = jnp.maximum(jnp.maximum(ms[0], ms[1]), ms[2])
            es = [jnp.exp(m - mx) for m in ms]
            num = sum(es[g] * osc[g, rows[g], :] for g in range(3))
            den = sum(es[g] * lsc[g, rows[g], :] for g in range(3))
            o_ref[0, 0, pl.ds(i4, CB), r * DH:(r + 1) * DH] = (num / den).astype(o_ref.dtype)
            return carry

        lax.fori_loop(0, SM // CB, comb, 0, unroll=2)


def _dilated_attention(groups, bias):
    B = groups[0].shape[0]
    S = groups[0].shape[3] * groups[0].shape[4]
    H, DH, MID = DIL_GROUP_HEADS, DIL_DH, DIL_MID
    assert [d for _, d in DIL_CONFIGS] == [1, MID, MID * MID] and S % (MID * MID * DIL_KWIN) == 0
    in_specs = [pl.BlockSpec((1, 1) + a.shape[2:], lambda b, j: (b, j, 0, 0, 0, 0)) for a in groups]
    in_specs.append(pl.BlockSpec((1, 3, 3, DIL_QBLK, DIL_KWIN), lambda b, j: (j, 0, 0, 0, 0)))
    args = list(groups) + [bias]
    vmem = (2 * 9 * S * DH * 2 + 2 * 9 * DIL_QBLK * DIL_KWIN * 4 + 2 * S * DH * 2
            + 3 * 3 * S * DH * 4 + (8 << 20))
    return pl.pallas_call(
        functools.partial(_dil_kernel, seq=S),
        grid=(B, H),
        in_specs=in_specs,
        out_specs=pl.BlockSpec((1, 1, S // MID, MID * DH), lambda b, j: (b, j, 0, 0)),
        out_shape=jax.ShapeDtypeStruct((B, H, S // MID, MID * DH), BF16),
        scratch_shapes=[pltpu.VMEM((3, S, DH), F32)] * 3,
        compiler_params=_cparams(vmem, 2),
        name="dilated_attention",
    )(*args)


def _merge_kernel(x_ref, yr_ref, yd_ref, gt_ref, wro_ref, wdo_ref, wout_ref, gn_ref, xo_ref, ho_ref, yd_scr):
    D = x_ref.shape[1]
    H, MID, DH = yd_scr.shape[0], DIL_MID, DIL_DH
    for r0 in range(0, x_ref.shape[0], MERGE_SUB):
        rows = slice(r0, r0 + MERGE_SUB)
        a = _dot(yr_ref[rows, :], wro_ref[...])
        for j in range(H):
            for r in range(MID):
                yd_scr[j, pl.ds(r0 + r, MERGE_SUB // MID, stride=MID), :] = (
                    yd_ref[0, j, r0 // MID:(r0 + MERGE_SUB) // MID, r * DH:(r + 1) * DH].astype(F32))
        yd = jnp.concatenate([yd_scr[j, rows, :].astype(BF16) for j in range(H)], axis=1)
        b = _dot(yd, wdo_ref[...])
        mg = gt_ref[rows, :D].astype(F32) * a + gt_ref[rows, D:].astype(F32) * b
        xn = x_ref[rows, :] + _dot(mg.astype(BF16), wout_ref[...])
        xo_ref[rows, :] = xn
        ho_ref[rows, :] = _rms(xn, gn_ref[...]).astype(ho_ref.dtype)


def _merge(x, yr, yd, gates, wro, wdo, wout, layer, g_next):
    T, D = x.shape
    tm = MERGE_TILE
    _, H, SM, W = yd.shape
    MID = DIL_MID
    nS = SM * MID // tm
    row = lambda n: pl.BlockSpec((tm, n), lambda i: (i, 0))
    vmem = ((wro[0].size + wdo[0].size + wout[0].size) * 2
            + 2 * tm * (D * 4 + yr.shape[1] * 2 + H * DIL_DH * 2 + gates.shape[1] * 2 + D * 4 + D * 2)
            + H * tm * DIL_DH * 4 + 6 * tm * D * 4)
    return pl.pallas_call(
        _merge_kernel,
        grid=(T // tm,),
        in_specs=[row(D), row(yr.shape[1]),
                  pl.BlockSpec((1, H, tm // MID, W), lambda i: (i // nS, 0, i % nS, 0)), row(gates.shape[1]),
                  _layer_weight(wro, layer), _layer_weight(wdo, layer), _layer_weight(wout, layer),
                  pl.BlockSpec((1, D), lambda i: (0, 0))],
        out_specs=[row(D), row(D)],
        out_shape=[jax.ShapeDtypeStruct((T, D), F32), jax.ShapeDtypeStruct((T, D), BF16)],
        scratch_shapes=[pltpu.VMEM((H, tm, DIL_DH), F32)],
        compiler_params=_cparams(vmem, 1),
        name="merge",
    )(x, yr, yd, gates, wro, wdo, wout, g_next.reshape(1, D))


def _t5_bucket(rel):
    nb = N_BUCKETS // 2
    max_exact = nb // 2
    ret = jnp.where(rel > 0, nb, 0)
    n = jnp.abs(rel)
    nf = jnp.maximum(n, 1).astype(F32)
    large = max_exact + (jnp.log(nf / max_exact) / math.log(MAX_DISTANCE / max_exact)
                         * (nb - max_exact)).astype(jnp.int32)
    large = jnp.minimum(large, nb - 1)
    return ret + jnp.where(n < max_exact, n, large)


def _dil_bias_tiles(rel_bias):
    a = jnp.arange(DIL_QBLK)[:, None]
    c = jnp.arange(DIL_KWIN)[None, :]
    per_group = []
    for gi, (_, d) in enumerate(DIL_CONFIGS):
        tiles = []
        for delta in (0, DIL_R, 2 * DIL_R):
            off = c - delta - a
            valid = jnp.abs(off) <= DIL_R
            bucket = _t5_bucket(jnp.clip(off, -DIL_R, DIL_R) * d)
            rb = rel_bias[:, gi * DIL_GROUP_HEADS:(gi + 1) * DIL_GROUP_HEADS].astype(F32)
            hit = bucket[..., None] == jnp.arange(N_BUCKETS)
            b = jnp.sum(jnp.where(hit[..., None], rb[None, None], 0.0), axis=2)
            tiles.append(jnp.where(valid[..., None], b, NEG))
        per_group.append(jnp.stack(tiles, axis=0))
    t = jnp.stack(per_group, axis=0)
    return jnp.transpose(t, (4, 0, 1, 2, 3))


def _ret_decay_tables(decay_exp):
    C = RET_CHUNK
    lg = jnp.log1p(-jnp.exp2(-decay_exp.astype(F32)))
    lf = lg[0][:, None, None]
    lb = lg[1][:, None, None]
    idx = jnp.arange(C, dtype=F32)
    diff = idx[:, None] - idx[None, :]
    dm = jnp.where(diff[None] >= 0, jnp.exp(lf * jnp.maximum(diff, 0.0)[None]),
                   jnp.exp(lb * jnp.maximum(-diff, 0.0)[None]))
    col = lambda v: jnp.broadcast_to(v[:, :, None], (v.shape[0], C, C))
    qdf = col(jnp.exp(lg[0][:, None] * (idx + 1.0)[None]))
    qdb = col(jnp.exp(lg[1][:, None] * (C - idx)[None]))
    kdf = col(jnp.exp(lg[0][:, None] * (C - 1.0 - idx)[None]))
    kdb = col(jnp.exp(lg[1][:, None] * idx[None]))
    dec = jnp.stack([dm, qdf, qdb, kdf, kdb], axis=1)
    cdec = jnp.stack([jnp.exp(lg[0] * C), jnp.exp(lg[1] * C)], axis=1)
    return dec, cdec


def _rope_tables(S):
    half = RET_DK // 2
    pos = jnp.arange(S, dtype=F32)
    inv = ROPE_BASE ** (-jnp.arange(half, dtype=F32) / half)
    ang = pos[:, None] * inv[None, :]
    cos, sin = jnp.cos(ang), jnp.sin(ang)
    return jnp.concatenate([cos, cos], axis=-1), jnp.concatenate([-sin, sin], axis=-1)


def kernel(x, rel_bias, norm_ffn1, ffn1_gate, ffn1_up, ffn1_down, norm_mix, w_in, b_gate,
           ret_decay_exp, ret_norm, w_ret_o, dil_q_norm, dil_k_norm, w_dil_o, w_out, norm_ffn2,
           ffn2_gate, ffn2_up, ffn2_down):
    B, S, D = x.shape
    depth = w_in.shape[0]
    T = B * S
    assert S % TOKEN_TILE == 0 and S % MERGE_TILE == 0
    ret_qk = RET_HEADS * RET_DK
    ret_v = RET_HEADS * RET_DV
    dil_w = DIL_GROUP_HEADS * len(DIL_CONFIGS) * DIL_DH
    o_rq, o_rv = 0, 2 * ret_qk
    o_dq = 2 * ret_qk + 2 * ret_v
    o_dk, o_dv = o_dq + dil_w, o_dq + 2 * dil_w
    o_gate = o_dq + 3 * dil_w
    gw = DIL_GROUP_HEADS * DIL_DH

    cosf, sinf = _rope_tables(S)
    bias_tiles = _dil_bias_tiles(rel_bias)
    bf = lambda a: a.astype(BF16)
    ffn1 = (bf(ffn1_gate), bf(ffn1_up), bf(ffn1_down))
    ffn2 = (bf(ffn2_gate), bf(ffn2_up), bf(ffn2_down))
    wqk, wvg, wgt = bf(w_in[:, :, o_rq:o_rv]), bf(w_in[:, :, o_rv:o_dq]), bf(w_in[:, :, o_gate:])
    wdil = [bf(jnp.concatenate([w_in[:, :, o + gi * gw:o + (gi + 1) * gw] for o in (o_dq, o_dk, o_dv)], axis=2))
            for gi in range(len(DIL_CONFIGS))]
    wro, wdo, wout = bf(w_ret_o), bf(w_dil_o), bf(w_out)

    xf = x.reshape(T, D)
    h = None
    for l in range(depth):
        xf, h = _ffn(xf, h, *ffn1, l, g_in=norm_ffn1[l], g_next=norm_mix[l])
        qk, vg, g0, g1, g2, gates = _projections(
            h, B, S, wqk, wvg, wdil, wgt, l, cosf, sinf,
            dil_q_norm[l].reshape(1, DIL_DH), dil_k_norm[l].reshape(1, DIL_DH), b_gate[l].reshape(1, 2 * D))

        dec, cdec = _ret_decay_tables(ret_decay_exp[l])
        yr = _retention(qk, vg, cdec, dec, ret_norm[l].reshape(RET_HEADS, 1, RET_DV))
        yd = _dilated_attention([g0, g1, g2], bias_tiles)

        xf, h = _merge(xf, yr.reshape(T, ret_v), yd, gates, wro, wdo, wout, l, norm_ffn2[l])
        if l + 1 < depth:
            xf, h = _ffn(xf, h, *ffn2, l, g_next=norm_ffn1[l + 1])
        else:
            xf, = _ffn(xf, h, *ffn2, l)
    return xf.reshape(B, S, D)
```

```python
import functools
import math

import jax
import jax.numpy as jnp
from jax import lax
from jax.experimental import pallas as pl
from jax.experimental.pallas import tpu as pltpu

F32 = jnp.float32
BF16 = jnp.bfloat16

V7X_LANES = 128
V7X_VMEM_BYTES = 64 * 1024 * 1024

EPS = 1e-6
ROPE_BASE = 10000.0
NEG = -1e30

RET_HEADS = 4
RET_DK = 128
RET_DV = 256
RET_CHUNK = 128
RET_BATCH = 8

DIL_CONFIGS = ((128, 1), (512, 4), (2048, 16))
DIL_GROUP_HEADS = 4
DIL_DH = 128
DIL_R = 64
DIL_QBLK = 2 * DIL_R
DIL_KWIN = 4 * DIL_R
DIL_BATCH = 8
DIL_MID = 4
N_BUCKETS = 32
MAX_DISTANCE = 1024

FFN_TILE = 1024
FFN_SUB = 512
PROJ_TILE = 1024
PROJ_SUB = 512
MERGE_TILE = 1024
MERGE_SUB = 512
FFN_CHUNK = 512
PROJ_CHUNK = 256


def _cparams(vmem_bytes, ngrid):
    return pltpu.CompilerParams(
        dimension_semantics=("arbitrary",) * ngrid,
        vmem_limit_bytes=int(min(vmem_bytes, V7X_VMEM_BYTES - (4 << 20))),
    )


def _layer_weight(w_all, layer):
    zeros = (0,) * (w_all.ndim - 1)
    return pl.BlockSpec((None,) + w_all.shape[1:], lambda *_: (layer,) + zeros,
                        pipeline_mode=pl.Buffered(1))


def _dot(a, b):
    return jnp.dot(a, b, preferred_element_type=F32)


def _dot_nt(a, b):
    return lax.dot_general(a, b, (((1,), (1,)), ((), ())), preferred_element_type=F32)


def _dot_tn(a, b):
    return lax.dot_general(a, b, (((0,), (0,)), ((), ())), preferred_element_type=F32)


def _rms(x32, g):
    ms = jnp.mean(x32 * x32, axis=-1, keepdims=True)
    return x32 * lax.rsqrt(ms + EPS) * g


def _sigmoid(x):
    return 0.5 * jnp.tanh(0.5 * x) + 0.5


def _ffn_kernel(x_ref, hg_ref, wg_ref, wu_ref, wd_ref, *rest, chunks, norm_in):
    for r0 in range(0, x_ref.shape[0], FFN_SUB):
        rows = slice(r0, r0 + FFN_SUB)
        h = _rms(x_ref[rows, :], hg_ref[...]).astype(BF16) if norm_in else hg_ref[rows, :]
        acc = None
        for c0, cw in chunks:
            g = _dot(h, wg_ref[:, c0:c0 + cw])
            u = _dot(h, wu_ref[:, c0:c0 + cw])
            a = (g * _sigmoid(g) * u).astype(BF16)
            d = _dot(a, wd_ref[c0:c0 + cw, :])
            acc = d if acc is None else acc + d
        xn = x_ref[rows, :] + 0.5 * acc
        if len(rest) == 3:
            gn_ref, xo_ref, ho_ref = rest
            ho_ref[rows, :] = _rms(xn, gn_ref[...]).astype(ho_ref.dtype)
        else:
            xo_ref, = rest
        xo_ref[rows, :] = xn


def _ffn(x, h, wg, wu, wd, layer, g_in=None, g_next=None):
    T, D = x.shape
    Fh = wg.shape[2]
    tm = FFN_TILE
    chunks = tuple((c0, min(FFN_CHUNK, Fh - c0)) for c0 in range(0, Fh, FFN_CHUNK))
    vmem = 3 * D * Fh * 2 + 2 * tm * D * (4 + 2 + 4 + 2) + 4 * tm * D * 4
    row = pl.BlockSpec((tm, D), lambda i: (i, 0))
    vec = pl.BlockSpec((1, D), lambda i: (0, 0))
    in_specs = [row, vec if h is None else row,
                _layer_weight(wg, layer), _layer_weight(wu, layer), _layer_weight(wd, layer)]
    args = [x, g_in.reshape(1, D) if h is None else h, wg, wu, wd]
    out_specs = [row]
    out_shape = [jax.ShapeDtypeStruct((T, D), F32)]
    if g_next is not None:
        in_specs.append(vec)
        args.append(g_next.reshape(1, D))
        out_specs.append(row)
        out_shape.append(jax.ShapeDtypeStruct((T, D), BF16))
    return pl.pallas_call(
        functools.partial(_ffn_kernel, chunks=chunks, norm_in=h is None),
        grid=(T // tm,),
        in_specs=in_specs,
        out_specs=out_specs,
        out_shape=out_shape,
        compiler_params=_cparams(vmem, 1),
        name="ffn",
    )(*args)


def _retqk_body(h_ref, w_ref, cos_ref, sin_ref, o_ref, r0, n):
    rows = slice(r0, r0 + n)
    cos = cos_ref[rows, :]
    sin = sin_ref[rows, :]
    per = PROJ_CHUNK // RET_DK
    for c in range(w_ref.shape[1] // PROJ_CHUNK):
        res = _dot(h_ref[rows, :], w_ref[:, c * PROJ_CHUNK:(c + 1) * PROJ_CHUNK])
        for k in range(per):
            t = res[:, k * RET_DK:(k + 1) * RET_DK]
            r = t * cos + pltpu.roll(t, RET_DK // 2, axis=1) * sin
            hd = c * per + k
            if hd >= RET_HEADS:
                r = r * (RET_DK ** -0.5)
            o_ref[0, hd % RET_HEADS, hd // RET_HEADS, rows, :] = r.astype(o_ref.dtype)


def _retvg_body(h_ref, w_ref, o_ref, r0, n):
    rows = slice(r0, r0 + n)
    per = PROJ_CHUNK // RET_DV
    for c in range(w_ref.shape[1] // PROJ_CHUNK):
        res = _dot(h_ref[rows, :], w_ref[:, c * PROJ_CHUNK:(c + 1) * PROJ_CHUNK])
        for k in range(per):
            t = res[:, k * RET_DV:(k + 1) * RET_DV]
            hd = c * per + k
            if hd >= RET_HEADS:
                t = t * _sigmoid(t)
            o_ref[0, hd % RET_HEADS, hd // RET_HEADS, rows, :] = t.astype(o_ref.dtype)


def _gate_body(h_ref, w_ref, b_ref, o_ref, r0, n):
    rows = slice(r0, r0 + n)
    for c0 in range(0, w_ref.shape[1], PROJ_CHUNK):
        res = _dot(h_ref[rows, :], w_ref[:, c0:c0 + PROJ_CHUNK]) + b_ref[:, c0:c0 + PROJ_CHUNK]
        o_ref[rows, c0:c0 + PROJ_CHUNK] = _sigmoid(res).astype(o_ref.dtype)


def _dil_body(h_ref, w_ref, qn, kn, o_ref, scr_ref, scr2_ref, d, r0, n):
    rows = slice(r0, r0 + n)
    MID = DIL_MID
    per = PROJ_CHUNK // DIL_DH
    for c in range(w_ref.shape[1] // PROJ_CHUNK):
        res = _dot(h_ref[rows, :], w_ref[:, c * PROJ_CHUNK:(c + 1) * PROJ_CHUNK])
        for k in range(per):
            t = res[:, k * DIL_DH:(k + 1) * DIL_DH]
            hd = c * per + k
            if hd < DIL_GROUP_HEADS:
                t = _rms(t, qn)
            elif hd < 2 * DIL_GROUP_HEADS:
                t = _rms(t, kn)
            if d == 1:
                o_ref[0, hd % DIL_GROUP_HEADS, hd // DIL_GROUP_HEADS, 0, rows, :] = t.astype(o_ref.dtype)
            else:
                scr_ref[hd, rows, :] = t
    if d == 1:
        return
    for hd in range(3 * DIL_GROUP_HEADS):
        out = o_ref.at[0, hd % DIL_GROUP_HEADS, hd // DIL_GROUP_HEADS]
        orow = slice(r0 // d, (r0 + n) // d)
        if d == MID:
            for r in range(d):
                out[r, orow, :] = scr_ref[hd, pl.ds(r0 + r, n // d, stride=d), :].astype(o_ref.dtype)
        else:
            assert d == MID * MID
            for r in range(MID):
                scr2_ref[r] = scr_ref[hd, pl.ds(r0 + r, n // MID, stride=MID), :]
            for r in range(MID):
                for a in range(MID):
                    out[MID * a + r, orow, :] = (
                        scr2_ref[r, pl.ds(a, n // d, stride=MID), :].astype(o_ref.dtype))


def _proj_ret_kernel(h_ref, wqk_ref, wvg_ref, wgt_ref, cos_ref, sin_ref, bg_ref, oqk_ref, ovg_ref, ogt_ref):
    for r0 in range(0, h_ref.shape[0], PROJ_SUB):
        _retqk_body(h_ref, wqk_ref, cos_ref, sin_ref, oqk_ref, r0, PROJ_SUB)
        _retvg_body(h_ref, wvg_ref, ovg_ref, r0, PROJ_SUB)
        _gate_body(h_ref, wgt_ref, bg_ref, ogt_ref, r0, PROJ_SUB)


def _proj_dil_kernel(h_ref, wd0_ref, wd1_ref, wd2_ref, qn_ref, kn_ref, od0_ref, od1_ref, od2_ref,
                     scr_ref, scr2_ref):
    qn = qn_ref[...] * (DIL_DH ** -0.5)
    kn = kn_ref[...]
    for i, r0 in enumerate(range(0, h_ref.shape[0], PROJ_SUB)):
        for w_ref, o_ref, (_, d) in zip((wd0_ref, wd1_ref, wd2_ref), (od0_ref, od1_ref, od2_ref), DIL_CONFIGS):
            _dil_body(h_ref, w_ref, qn, kn, o_ref, scr_ref, scr2_ref.at[i], d, r0, PROJ_SUB)


def _projections(h, B, S, wqk, wvg, wdil, wgt, layer, cosf, sinf, qn, kn, bg):
    T, D = h.shape
    tm = PROJ_TILE
    nS = S // tm
    nh_dil = 3 * DIL_GROUP_HEADS
    full = lambda a: pl.BlockSpec(a.shape, lambda i: (0,) * a.ndim)
    lw = lambda w: _layer_weight(w, layer)
    hrow = pl.BlockSpec((tm, D), lambda i: (i, 0))
    seq_idx = lambda i: (i // nS, 0, 0, i % nS, 0)
    width = lambda ws: sum(w.shape[2] for w in ws)
    n_ret = width([wqk, wvg, wgt])
    qk, vg, gates = pl.pallas_call(
        _proj_ret_kernel,
        grid=(T // tm,),
        in_specs=[hrow, lw(wqk), lw(wvg), lw(wgt),
                  pl.BlockSpec((tm, RET_DK), lambda i: (i % nS, 0)),
                  pl.BlockSpec((tm, RET_DK), lambda i: (i % nS, 0)), full(bg)],
        out_specs=[pl.BlockSpec((1, RET_HEADS, 2, tm, RET_DK), seq_idx),
                   pl.BlockSpec((1, RET_HEADS, 2, tm, RET_DV), seq_idx),
                   pl.BlockSpec((tm, wgt.shape[2]), lambda i: (i, 0))],
        out_shape=[jax.ShapeDtypeStruct((B, RET_HEADS, 2, S, RET_DK), BF16),
                   jax.ShapeDtypeStruct((B, RET_HEADS, 2, S, RET_DV), BF16),
                   jax.ShapeDtypeStruct((T, wgt.shape[2]), BF16)],
        compiler_params=_cparams(D * n_ret * 2 + 2 * tm * D * 2 + 2 * tm * n_ret * 2
                                 + 8 * PROJ_SUB * PROJ_CHUNK * 4 + 4 * tm * RET_DK * 4, 1),
        name="proj_ret",
    )(h, wqk, wvg, wgt, cosf, sinf, bg)
    n_dil = width(wdil)
    groups = pl.pallas_call(
        _proj_dil_kernel,
        grid=(T // tm,),
        in_specs=[hrow] + [lw(w) for w in wdil] + [full(qn), full(kn)],
        out_specs=[pl.BlockSpec((1, DIL_GROUP_HEADS, 3, d, tm // d, DIL_DH),
                                lambda i: (i // nS, 0, 0, 0, i % nS, 0)) for _, d in DIL_CONFIGS],
        out_shape=[jax.ShapeDtypeStruct((B, DIL_GROUP_HEADS, 3, d, S // d, DIL_DH), BF16)
                   for _, d in DIL_CONFIGS],
        scratch_shapes=[pltpu.VMEM((nh_dil, tm, DIL_DH), F32),
                        pltpu.VMEM((tm // PROJ_SUB, DIL_MID, PROJ_SUB // DIL_MID, DIL_DH), F32)],
        compiler_params=_cparams(D * n_dil * 2 + 2 * tm * D * 2 + 2 * tm * n_dil * 2
                                 + (nh_dil + 1) * tm * DIL_DH * 4 + 8 * PROJ_SUB * PROJ_CHUNK * 4, 1),
        name="proj_dil",
    )(h, *wdil, qn, kn)
    return qk, vg, groups, gates


def _ret_kernel(cdec_ref, qk_ref, vg_ref, dec_ref, ng_ref, o_ref,
                kvf_scr, kvb_scr, fst_scr, bst_scr, *, nchunks):
    C = RET_CHUNK
    q_ref, k_ref = qk_ref.at[0, 0, 0], qk_ref.at[0, 0, 1]
    v_ref, sg_ref = vg_ref.at[0, 0, 0], vg_ref.at[0, 0, 1]
    hd = pl.program_id(1)
    cf = cdec_ref[hd, 0]
    cb = cdec_ref[hd, 1]

    def rows(n):
        return pl.ds(pl.multiple_of(n * C, C), C)

    def kv_body(n, carry):
        kn = k_ref[rows(n), :].astype(F32)
        vn = v_ref[rows(n), :]
        kvf_scr[n] = _dot_tn((kn * dec_ref[0, 3]).astype(BF16), vn)
        kvb_scr[n] = _dot_tn((kn * dec_ref[0, 4]).astype(BF16), vn)
        return carry

    lax.fori_loop(0, nchunks, kv_body, 0, unroll=16)

    def scan_f(n, st):
        fst_scr[n] = st.astype(BF16)
        return cf * st + kvf_scr[n]

    lax.fori_loop(0, nchunks, scan_f, jnp.zeros((RET_DK, RET_DV), F32))

    def scan_b(i, st):
        n = nchunks - 1 - i
        bst_scr[n] = st.astype(BF16)
        return cb * st + kvb_scr[n]

    lax.fori_loop(0, nchunks, scan_b, jnp.zeros((RET_DK, RET_DV), F32))

    G = RET_BATCH

    def out_body(t, carry):
        rw = pl.ds(pl.multiple_of(t * (G * C), G * C), G * C)
        cs = pl.ds(pl.multiple_of(t * G, G), G)
        q = q_ref[rw, :].reshape(G, C, RET_DK)
        k = k_ref[rw, :].reshape(G, C, RET_DK)
        v = v_ref[rw, :].reshape(G, C, RET_DV)
        q32 = q.astype(F32)
        s = jnp.einsum('gcd,gjd->gcj', q, k, preferred_element_type=F32) * dec_ref[0, 0]
        lhs = jnp.concatenate([s.astype(BF16), (q32 * dec_ref[0, 1]).astype(BF16),
                               (q32 * dec_ref[0, 2]).astype(BF16)], axis=2)
        rhs = jnp.concatenate([v, fst_scr[cs], bst_scr[cs]], axis=1)
        o = jnp.einsum('gck,gke->gce', lhs, rhs, preferred_element_type=F32)
        r = lax.rsqrt(jnp.mean(o * o, axis=-1, keepdims=True) + EPS)
        y = (o * r * ng_ref[0]).reshape(G * C, RET_DV) * sg_ref[rw, :].astype(F32)
        o_ref[0, rw, :] = y.astype(o_ref.dtype)
        return carry

    lax.fori_loop(0, nchunks // G, out_body, 0, unroll=2)


def _retention(qk, vg, cdec, dec, ng):
    B, H, _, S, _ = qk.shape
    nchunks = S // RET_CHUNK
    vmem = (2 * (2 * S * RET_DK * 2 + 2 * S * RET_DV * 2 + S * RET_DV * 2)
            + nchunks * RET_DK * RET_DV * (4 + 4 + 2 + 2) + (8 << 20))
    return pl.pallas_call(
        functools.partial(_ret_kernel, nchunks=nchunks),
        grid=(B, H),
        in_specs=[pl.BlockSpec(memory_space=pltpu.SMEM),
                  pl.BlockSpec((1, 1, 2, S, RET_DK), lambda b, h: (b, h, 0, 0, 0)),
                  pl.BlockSpec((1, 1, 2, S, RET_DV), lambda b, h: (b, h, 0, 0, 0)),
                  pl.BlockSpec((1, 5, RET_CHUNK, RET_CHUNK), lambda b, h: (h, 0, 0, 0)),
                  pl.BlockSpec((1, 1, RET_DV), lambda b, h: (h, 0, 0))],
        out_specs=pl.BlockSpec((1, S, RET_DV), lambda b, h: (b, 0, h)),
        out_shape=jax.ShapeDtypeStruct((B, S, H * RET_DV), BF16),
        scratch_shapes=[pltpu.VMEM((nchunks, RET_DK, RET_DV), F32),
                        pltpu.VMEM((nchunks, RET_DK, RET_DV), F32),
                        pltpu.VMEM((nchunks, RET_DK, RET_DV), BF16),
                        pltpu.VMEM((nchunks, RET_DK, RET_DV), BF16)],
        compiler_params=_cparams(vmem, 2),
        name="retention",
    )(cdec, qk, vg, dec, ng)


def _dil_kernel(g0_ref, g1_ref, g2_ref, bias_ref, o_ref, osc, lsc, msc, *, seq):
    QB, KW, DH, MID, G = DIL_QBLK, DIL_KWIN, DIL_DH, DIL_MID, DIL_BATCH
    SM = seq // MID
    (q0, k0, v0), (q1, k1, v1), (q2, k2, v2) = (
        tuple(g.at[0, 0, part] for part in range(3)) for g in (g0_ref, g1_ref, g2_ref))
    dils = tuple(d for _, d in DIL_CONFIGS)
    ones = jnp.ones((KW, DH), BF16)

    def window(n, L):
        nblk = L // QB
        if isinstance(n, int):
            i0 = n * QB
            return i0, min(max(i0 - DIL_R, 0), L - KW), (0 if n == 0 else 2 if n == nblk - 1 else 1)
        i0 = pl.multiple_of(n * QB, QB)
        w0 = pl.multiple_of(jnp.clip(i0 - DIL_R, 0, L - KW), DIL_R)
        return i0, w0, jnp.where(n == 0, 0, jnp.where(n == nblk - 1, 2, 1))

    def attend(g, tiles):
        q = jnp.stack([t[0] for t in tiles])
        k = jnp.stack([t[1] for t in tiles])
        v = jnp.stack([jnp.concatenate([t[2], ones], axis=1) for t in tiles])
        s = jnp.einsum('gqd,gkd->gqk', q, k, preferred_element_type=F32)
        s = s + jnp.stack([bias_ref[0, g, t[3]] for t in tiles])
        m = jnp.max(s, axis=-1, keepdims=True)
        p = jnp.exp(s - m).astype(BF16)
        ol = jnp.einsum('gqk,gke->gqe', p, v, preferred_element_type=F32)
        for i, t in enumerate(tiles):
            osc[g, t[4], :] = ol[i, :, :DH]
            lsc[g, t[4], :] = ol[i, :, DH:]
            msc[g, t[4], :] = jnp.broadcast_to(m[i], (QB, DH))

    def g0_body(t, carry):
        tiles = []
        for i in range(G):
            i0, w0, var = window(t * G + i, seq)
            tiles.append((q0[0, pl.ds(i0, QB), :], k0[0, pl.ds(w0, KW), :],
                          v0[0, pl.ds(w0, KW), :], var, pl.ds(i0, QB)))
        attend(0, tiles)
        return carry

    lax.fori_loop(0, seq // (QB * G), g0_body, 0, unroll=2)

    per1 = SM // (QB * G)

    def g1_body(t, carry):
        r = t // per1
        tiles = []
        for i in range(G):
            i0, w0, var = window(i if per1 == 1 else (t % per1) * G + i, SM)
            tiles.append((q1[r, pl.ds(i0, QB), :], k1[r, pl.ds(w0, KW), :],
                          v1[r, pl.ds(w0, KW), :], var,
                          pl.ds(pl.multiple_of(r * SM + i0, QB), QB)))
        attend(1, tiles)
        return carry

    lax.fori_loop(0, MID * per1, g1_body, 0, unroll=2)

    L2 = seq // dils[2]
    pend = []
    for a in range(dils[2] // MID):
        for r in range(MID):
            res = MID * a + r
            for n in range(L2 // QB):
                i0, w0, var = window(n, L2)
                pend.append((q2[res, i0:i0 + QB, :], k2[res, w0:w0 + KW, :],
                             v2[res, w0:w0 + KW, :], var,
                             pl.ds(r * SM + MID * i0 + a, QB, stride=MID)))
                if len(pend) == G:
                    attend(2, pend)
                    pend = []
    assert not pend

    CB = 256
    for r in range(MID):
        def comb(t, carry, r=r):
            i4 = pl.multiple_of(t * CB, CB)
            rows = (pl.ds(r + MID * i4, CB, stride=MID), pl.ds(r * SM + i4, CB), pl.ds(r * SM + i4, CB))
            ms = [msc[g, rows[g], :] for g in range(3)]
            mx = jnp.maximum(jnp.maximum(ms[0], ms[1]), ms[2])
            es = [jnp.exp(m - mx) for m in ms]
            num = sum(es[g] * osc[g, rows[g], :] for g in range(3))
            den = sum(es[g] * lsc[g, rows[g], :] for g in range(3))
            o_ref[0, 0, pl.ds(i4, CB), r * DH:(r + 1) * DH] = (num / den).astype(o_ref.dtype)
            return carry

        lax.fori_loop(0, SM // CB, comb, 0, unroll=2)


def _dilated_attention(groups, bias):
    B = groups[0].shape[0]
    S = groups[0].shape[3] * groups[0].shape[4]
    H, DH, MID = DIL_GROUP_HEADS, DIL_DH, DIL_MID
    assert [d for _, d in DIL_CONFIGS] == [1, MID, MID * MID] and S % (MID * MID * DIL_KWIN) == 0
    in_specs = [pl.BlockSpec((1, 1) + a.shape[2:], lambda b, j: (b, j, 0, 0, 0, 0)) for a in groups]
    in_specs.append(pl.BlockSpec((1, 3, 3, DIL_QBLK, DIL_KWIN), lambda b, j: (j, 0, 0, 0, 0)))
    args = list(groups) + [bias]
    vmem = (2 * 9 * S * DH * 2 + 2 * 9 * DIL_QBLK * DIL_KWIN * 4 + 2 * S * DH * 2
            + 3 * 3 * S * DH * 4 + (8 << 20))
    return pl.pallas_call(
        functools.partial(_dil_kernel, seq=S),
        grid=(B, H),
        in_specs=in_specs,
        out_specs=pl.BlockSpec((1, 1, S // MID, MID * DH), lambda b, j: (b, j, 0, 0)),
        out_shape=jax.ShapeDtypeStruct((B, H, S // MID, MID * DH), BF16),
        scratch_shapes=[pltpu.VMEM((3, S, DH), F32)] * 3,
        compiler_params=_cparams(vmem, 2),
        name="dilated_attention",
    )(*args)


def _merge_kernel(x_ref, yr_ref, yd_ref, gt_ref, wro_ref, wdo_ref, wout_ref, gn_ref, xo_ref, ho_ref, yd_scr):
    D = x_ref.shape[1]
    H, MID, DH = yd_scr.shape[0], DIL_MID, DIL_DH
    for r0 in range(0, x_ref.shape[0], MERGE_SUB):
        rows = slice(r0, r0 + MERGE_SUB)
        a = _dot(yr_ref[rows, :], wro_ref[...])
        for j in range(H):
            for r in range(MID):
                yd_scr[j, pl.ds(r0 + r, MERGE_SUB // MID, stride=MID), :] = (
                    yd_ref[0, j, r0 // MID:(r0 + MERGE_SUB) // MID, r * DH:(r + 1) * DH].astype(F32))
        yd = jnp.concatenate([yd_scr[j, rows, :].astype(BF16) for j in range(H)], axis=1)
        b = _dot(yd, wdo_ref[...])
        mg = gt_ref[rows, :D].astype(F32) * a + gt_ref[rows, D:].astype(F32) * b
        xn = x_ref[rows, :] + _dot(mg.astype(BF16), wout_ref[...])
        xo_ref[rows, :] = xn
        ho_ref[rows, :] = _rms(xn, gn_ref[...]).astype(ho_ref.dtype)


def _merge(x, yr, yd, gates, wro, wdo, wout, layer, g_next):
    T, D = x.shape
    tm = MERGE_TILE
    _, H, SM, W = yd.shape
    MID = DIL_MID
    nS = SM * MID // tm
    row = lambda n: pl.BlockSpec((tm, n), lambda i: (i, 0))
    vmem = ((wro[0].size + wdo[0].size + wout[0].size) * 2
            + 2 * tm * (D * 4 + yr.shape[1] * 2 + H * DIL_DH * 2 + gates.shape[1] * 2 + D * 4 + D * 2)
            + H * tm * DIL_DH * 4 + 6 * tm * D * 4)
    return pl.pallas_call(
        _merge_kernel,
        grid=(T // tm,),
        in_specs=[row(D), row(yr.shape[1]),
                  pl.BlockSpec((1, H, tm // MID, W), lambda i: (i // nS, 0, i % nS, 0)), row(gates.shape[1]),
                  _layer_weight(wro, layer), _layer_weight(wdo, layer), _layer_weight(wout, layer),
                  pl.BlockSpec((1, D), lambda i: (0, 0))],
        out_specs=[row(D), row(D)],
        out_shape=[jax.ShapeDtypeStruct((T, D), F32), jax.ShapeDtypeStruct((T, D), BF16)],
        scratch_shapes=[pltpu.VMEM((H, tm, DIL_DH), F32)],
        compiler_params=_cparams(vmem, 1),
        name="merge",
    )(x, yr, yd, gates, wro, wdo, wout, g_next.reshape(1, D))


def _t5_bucket(rel):
    nb = N_BUCKETS // 2
    max_exact = nb // 2
    ret = jnp.where(rel > 0, nb, 0)
    n = jnp.abs(rel)
    nf = jnp.maximum(n, 1).astype(F32)
    large = max_exact + (jnp.log(nf / max_exact) / math.log(MAX_DISTANCE / max_exact)
                         * (nb - max_exact)).astype(jnp.int32)
    large = jnp.minimum(large, nb - 1)
    return ret + jnp.where(n < max_exact, n, large)


def _dil_bias_tiles(rel_bias):
    Q, K, R = DIL_QBLK, DIL_KWIN, DIL_R
    P = Q + K
    per_group = []
    for gi, (_, d) in enumerate(DIL_CONFIGS):
        rb = rel_bias[:, gi * DIL_GROUP_HEADS:(gi + 1) * DIL_GROUP_HEADS].astype(F32)
        tiles = []
        for delta in (0, R, 2 * R):
            off = jnp.arange(P) - (Q - 1) - delta
            bucket = _t5_bucket(jnp.clip(off, -R, R) * d)
            hit = bucket[:, None] == jnp.arange(N_BUCKETS)
            w = jnp.sum(jnp.where(hit[..., None], rb[None], 0.0), axis=1)
            w = jnp.where((jnp.abs(off) <= R)[:, None], w, NEG)
            rows = jnp.tile(w, (Q + 1, 1))[:Q * (P + 1)].reshape(Q, P + 1, -1)
            tiles.append(jnp.flip(rows[:, :K], axis=0))
        per_group.append(jnp.stack(tiles, axis=0))
    t = jnp.stack(per_group, axis=0)
    return jnp.transpose(t, (4, 0, 1, 2, 3))


def _ret_decay_tables(decay_exp):
    C = RET_CHUNK
    lg = jnp.log1p(-jnp.exp2(-decay_exp.astype(F32)))
    lf = lg[0][:, None, None]
    lb = lg[1][:, None, None]
    idx = jnp.arange(C, dtype=F32)
    diff = idx[:, None] - idx[None, :]
    dm = jnp.where(diff[None] >= 0, jnp.exp(lf * jnp.maximum(diff, 0.0)[None]),
                   jnp.exp(lb * jnp.maximum(-diff, 0.0)[None]))
    col = lambda v: jnp.broadcast_to(v[:, :, None], (v.shape[0], C, C))
    qdf = col(jnp.exp(lg[0][:, None] * (idx + 1.0)[None]))
    qdb = col(jnp.exp(lg[1][:, None] * (C - idx)[None]))
    kdf = col(jnp.exp(lg[0][:, None] * (C - 1.0 - idx)[None]))
    kdb = col(jnp.exp(lg[1][:, None] * idx[None]))
    dec = jnp.stack([dm, qdf, qdb, kdf, kdb], axis=1)
    cdec = jnp.stack([jnp.exp(lg[0] * C), jnp.exp(lg[1] * C)], axis=1)
    return dec, cdec


def _rope_tables(S):
    half = RET_DK // 2
    pos = jnp.arange(S, dtype=F32)
    inv = ROPE_BASE ** (-jnp.arange(half, dtype=F32) / half)
    ang = pos[:, None] * inv[None, :]
    cos, sin = jnp.cos(ang), jnp.sin(ang)
    return jnp.concatenate([cos, cos], axis=-1), jnp.concatenate([-sin, sin], axis=-1)


def kernel(x, rel_bias, norm_ffn1, ffn1_gate, ffn1_up, ffn1_down, norm_mix, w_in, b_gate,
           ret_decay_exp, ret_norm, w_ret_o, dil_q_norm, dil_k_norm, w_dil_o, w_out, norm_ffn2,
           ffn2_gate, ffn2_up, ffn2_down):
    B, S, D = x.shape
    depth = w_in.shape[0]
    T = B * S
    assert S % PROJ_TILE == 0 and S % MERGE_TILE == 0
    ret_qk = RET_HEADS * RET_DK
    ret_v = RET_HEADS * RET_DV
    dil_w = DIL_GROUP_HEADS * len(DIL_CONFIGS) * DIL_DH
    o_rq, o_rv = 0, 2 * ret_qk
    o_dq = 2 * ret_qk + 2 * ret_v
    o_dk, o_dv = o_dq + dil_w, o_dq + 2 * dil_w
    o_gate = o_dq + 3 * dil_w
    gw = DIL_GROUP_HEADS * DIL_DH

    cosf, sinf = _rope_tables(S)
    bias_tiles = _dil_bias_tiles(rel_bias)
    bf = lambda a: a.astype(BF16)
    ffn1 = (bf(ffn1_gate), bf(ffn1_up), bf(ffn1_down))
    ffn2 = (bf(ffn2_gate), bf(ffn2_up), bf(ffn2_down))
    wqk, wvg, wgt = bf(w_in[:, :, o_rq:o_rv]), bf(w_in[:, :, o_rv:o_dq]), bf(w_in[:, :, o_gate:])
    wdil = [bf(jnp.concatenate([w_in[:, :, o + gi * gw:o + (gi + 1) * gw] for o in (o_dq, o_dk, o_dv)], axis=2))
            for gi in range(len(DIL_CONFIGS))]
    wro, wdo, wout = bf(w_ret_o), bf(w_dil_o), bf(w_out)

    xf = x.reshape(T, D)
    h = None
    for l in range(depth):
        xf, h = _ffn(xf, h, *ffn1, l, g_in=norm_ffn1[l], g_next=norm_mix[l])
        qk, vg, groups, gates = _projections(
            h, B, S, wqk, wvg, wdil, wgt, l, cosf, sinf,
            dil_q_norm[l].reshape(1, DIL_DH), dil_k_norm[l].reshape(1, DIL_DH), b_gate[l].reshape(1, 2 * D))

        dec, cdec = _ret_decay_tables(ret_decay_exp[l])
        yr = _retention(qk, vg, cdec, dec, ret_norm[l].reshape(RET_HEADS, 1, RET_DV))
        yd = _dilated_attention(groups, bias_tiles)

        xf, h = _merge(xf, yr.reshape(T, ret_v), yd, gates, wro, wdo, wout, l, norm_ffn2[l])
        if l + 1 < depth:
            xf, h = _ffn(xf, h, *ffn2, l, g_next=norm_ffn1[l + 1])
        else:
            xf, = _ffn(xf, h, *ffn2, l)
    return xf.reshape(B, S, D)
```

```python
import functools
import math

import jax
import jax.numpy as jnp
from jax import lax
from jax.experimental import pallas as pl
from jax.experimental.pallas import tpu as pltpu

F32 = jnp.float32
BF16 = jnp.bfloat16

V7X_VMEM_BYTES = 64 * 1024 * 1024
VMEM_LIMIT_CAP = V7X_VMEM_BYTES - (4 << 20)
VMEM_TEMPS = 8 << 20

EPS = 1e-6
ROPE_BASE = 10000.0
NEG = -1e30

RET_HEADS = 4
RET_DK = 128
RET_DV = 256
RET_CHUNK = 128
RET_BATCH = 8

DIL_CONFIGS = ((128, 1), (512, 4), (2048, 16))
DIL_GROUP_HEADS = 4
DIL_DH = 128
DIL_R = 64
DIL_QBLK = 2 * DIL_R
DIL_KWIN = 4 * DIL_R
DIL_BATCH = 8
DIL_MID = 4
DIL_COMB_ROWS = 256
N_BUCKETS = 32
MAX_DISTANCE = 1024

FFN_TILE = 1024
FFN_SUB = 512
PROJ_TILE = 1024
PROJ_SUB = 512
MERGE_TILE = 1024
MERGE_SUB = 512
FFN_CHUNK = 512
PROJ_CHUNK = 256


def _cparams(vmem_bytes, ngrid):
    return pltpu.CompilerParams(
        dimension_semantics=("arbitrary",) * ngrid,
        vmem_limit_bytes=int(min(vmem_bytes, VMEM_LIMIT_CAP)),
    )


def _layer_weight(w_all, layer):
    zeros = (0,) * (w_all.ndim - 1)
    return pl.BlockSpec((None,) + w_all.shape[1:], lambda *_: (layer,) + zeros,
                        pipeline_mode=pl.Buffered(1))


def _dot(a, b):
    return jnp.dot(a, b, preferred_element_type=F32)


def _dot_nt(a, b):
    return lax.dot_general(a, b, (((1,), (1,)), ((), ())), preferred_element_type=F32)


def _dot_tn(a, b):
    return lax.dot_general(a, b, (((0,), (0,)), ((), ())), preferred_element_type=F32)


def _rms(x32, g):
    ms = jnp.mean(x32 * x32, axis=-1, keepdims=True)
    return x32 * lax.rsqrt(ms + EPS) * g


def _sigmoid(x):
    return 0.5 * jnp.tanh(0.5 * x) + 0.5


def _ffn_kernel(x_ref, hg_ref, wg_ref, wu_ref, wd_ref, *rest, chunks, norm_in):
    for r0 in range(0, x_ref.shape[0], FFN_SUB):
        rows = slice(r0, r0 + FFN_SUB)
        h = _rms(x_ref[rows, :], hg_ref[...]).astype(BF16) if norm_in else hg_ref[rows, :]
        acc = None
        for c0, cw in chunks:
            g = _dot(h, wg_ref[:, c0:c0 + cw])
            u = _dot(h, wu_ref[:, c0:c0 + cw])
            a = (g * _sigmoid(g) * u).astype(BF16)
            d = _dot(a, wd_ref[c0:c0 + cw, :])
            acc = d if acc is None else acc + d
        xn = x_ref[rows, :] + 0.5 * acc
        if len(rest) == 3:
            gn_ref, xo_ref, ho_ref = rest
            ho_ref[rows, :] = _rms(xn, gn_ref[...]).astype(ho_ref.dtype)
        else:
            xo_ref, = rest
        xo_ref[rows, :] = xn


def _ffn(x, h, wg, wu, wd, layer, g_in=None, g_next=None):
    assert (h is None) == (g_in is not None)
    T, D = x.shape
    Fh = wg.shape[2]
    tm = FFN_TILE
    chunks = tuple((c0, min(FFN_CHUNK, Fh - c0)) for c0 in range(0, Fh, FFN_CHUNK))
    vmem = 3 * D * Fh * 2 + 2 * tm * D * (4 + 2 + 4 + 2) + 4 * tm * D * 4
    row = pl.BlockSpec((tm, D), lambda i: (i, 0))
    vec = pl.BlockSpec((1, D), lambda i: (0, 0))
    in_specs = [row, vec if h is None else row,
                _layer_weight(wg, layer), _layer_weight(wu, layer), _layer_weight(wd, layer)]
    args = [x, g_in.reshape(1, D) if h is None else h, wg, wu, wd]
    out_specs = [row]
    out_shape = [jax.ShapeDtypeStruct((T, D), F32)]
    if g_next is not None:
        in_specs.append(vec)
        args.append(g_next.reshape(1, D))
        out_specs.append(row)
        out_shape.append(jax.ShapeDtypeStruct((T, D), BF16))
    return pl.pallas_call(
        functools.partial(_ffn_kernel, chunks=chunks, norm_in=h is None),
        grid=(T // tm,),
        in_specs=in_specs,
        out_specs=out_specs,
        out_shape=out_shape,
        compiler_params=_cparams(vmem, 1),
        name="ffn",
    )(*args)


def _retqk_body(h_ref, w_ref, cos_ref, sin_ref, o_ref, r0, n):
    rows = slice(r0, r0 + n)
    cos = cos_ref[rows, :]
    sin = sin_ref[rows, :]
    per = PROJ_CHUNK // RET_DK
    for c in range(w_ref.shape[1] // PROJ_CHUNK):
        res = _dot(h_ref[rows, :], w_ref[:, c * PROJ_CHUNK:(c + 1) * PROJ_CHUNK])
        for k in range(per):
            t = res[:, k * RET_DK:(k + 1) * RET_DK]
            r = t * cos + pltpu.roll(t, RET_DK // 2, axis=1) * sin
            hd = c * per + k
            if hd >= RET_HEADS:
                r = r * (RET_DK ** -0.5)
            o_ref[0, hd % RET_HEADS, hd // RET_HEADS, rows, :] = r.astype(o_ref.dtype)


def _retvg_body(h_ref, w_ref, o_ref, r0, n):
    rows = slice(r0, r0 + n)
    per = PROJ_CHUNK // RET_DV
    for c in range(w_ref.shape[1] // PROJ_CHUNK):
        res = _dot(h_ref[rows, :], w_ref[:, c * PROJ_CHUNK:(c + 1) * PROJ_CHUNK])
        for k in range(per):
            t = res[:, k * RET_DV:(k + 1) * RET_DV]
            hd = c * per + k
            if hd >= RET_HEADS:
                t = t * _sigmoid(t)
            o_ref[0, hd % RET_HEADS, hd // RET_HEADS, rows, :] = t.astype(o_ref.dtype)


def _gate_body(h_ref, w_ref, b_ref, o_ref, r0, n):
    rows = slice(r0, r0 + n)
    for c0 in range(0, w_ref.shape[1], PROJ_CHUNK):
        res = _dot(h_ref[rows, :], w_ref[:, c0:c0 + PROJ_CHUNK]) + b_ref[:, c0:c0 + PROJ_CHUNK]
        o_ref[rows, c0:c0 + PROJ_CHUNK] = _sigmoid(res).astype(o_ref.dtype)


def _dil_body(h_ref, w_ref, qn, kn, o_ref, scr_ref, scr2_ref, d, r0, n):
    rows = slice(r0, r0 + n)
    MID = DIL_MID
    per = PROJ_CHUNK // DIL_DH
    for c in range(w_ref.shape[1] // PROJ_CHUNK):
        res = _dot(h_ref[rows, :], w_ref[:, c * PROJ_CHUNK:(c + 1) * PROJ_CHUNK])
        for k in range(per):
            t = res[:, k * DIL_DH:(k + 1) * DIL_DH]
            hd = c * per + k
            if hd < DIL_GROUP_HEADS:
                t = _rms(t, qn)
            elif hd < 2 * DIL_GROUP_HEADS:
                t = _rms(t, kn)
            if d == 1:
                o_ref[0, hd % DIL_GROUP_HEADS, hd // DIL_GROUP_HEADS, 0, rows, :] = t.astype(o_ref.dtype)
            else:
                scr_ref[hd, rows, :] = t
    if d == 1:
        return
    for hd in range(3 * DIL_GROUP_HEADS):
        out = o_ref.at[0, hd % DIL_GROUP_HEADS, hd // DIL_GROUP_HEADS]
        orow = slice(r0 // d, (r0 + n) // d)
        if d == MID:
            for r in range(d):
                out[r, orow, :] = scr_ref[hd, pl.ds(r0 + r, n // d, stride=d), :].astype(o_ref.dtype)
        else:
            assert d == MID * MID
            for r in range(MID):
                scr2_ref[r] = scr_ref[hd, pl.ds(r0 + r, n // MID, stride=MID), :]
            for r in range(MID):
                for a in range(MID):
                    out[MID * a + r, orow, :] = (
                        scr2_ref[r, pl.ds(a, n // d, stride=MID), :].astype(o_ref.dtype))


def _proj_ret_kernel(h_ref, wqk_ref, wvg_ref, wgt_ref, cos_ref, sin_ref, bg_ref, oqk_ref, ovg_ref, ogt_ref):
    for r0 in range(0, h_ref.shape[0], PROJ_SUB):
        _retqk_body(h_ref, wqk_ref, cos_ref, sin_ref, oqk_ref, r0, PROJ_SUB)
        _retvg_body(h_ref, wvg_ref, ovg_ref, r0, PROJ_SUB)
        _gate_body(h_ref, wgt_ref, bg_ref, ogt_ref, r0, PROJ_SUB)


def _proj_dil_kernel(h_ref, wd0_ref, wd1_ref, wd2_ref, qn_ref, kn_ref, od0_ref, od1_ref, od2_ref,
                     scr_ref, scr2_ref):
    qn = qn_ref[...] * (DIL_DH ** -0.5)
    kn = kn_ref[...]
    for i, r0 in enumerate(range(0, h_ref.shape[0], PROJ_SUB)):
        for w_ref, o_ref, (_, d) in zip((wd0_ref, wd1_ref, wd2_ref), (od0_ref, od1_ref, od2_ref), DIL_CONFIGS):
            _dil_body(h_ref, w_ref, qn, kn, o_ref, scr_ref, scr2_ref.at[i], d, r0, PROJ_SUB)


def _projections(h, B, S, wqk, wvg, wdil, wgt, layer, cosf, sinf, qn, kn, bg):
    T, D = h.shape
    tm = PROJ_TILE
    nS = S // tm
    nh_dil = 3 * DIL_GROUP_HEADS
    full = lambda a: pl.BlockSpec(a.shape, lambda i: (0,) * a.ndim)
    lw = lambda w: _layer_weight(w, layer)
    hrow = pl.BlockSpec((tm, D), lambda i: (i, 0))
    seq_idx = lambda i: (i // nS, 0, 0, i % nS, 0)
    width = lambda ws: sum(w.shape[2] for w in ws)
    n_ret = width([wqk, wvg, wgt])
    qk, vg, gates = pl.pallas_call(
        _proj_ret_kernel,
        grid=(T // tm,),
        in_specs=[hrow, lw(wqk), lw(wvg), lw(wgt),
                  pl.BlockSpec((tm, RET_DK), lambda i: (i % nS, 0)),
                  pl.BlockSpec((tm, RET_DK), lambda i: (i % nS, 0)), full(bg)],
        out_specs=[pl.BlockSpec((1, RET_HEADS, 2, tm, RET_DK), seq_idx),
                   pl.BlockSpec((1, RET_HEADS, 2, tm, RET_DV), seq_idx),
                   pl.BlockSpec((tm, wgt.shape[2]), lambda i: (i, 0))],
        out_shape=[jax.ShapeDtypeStruct((B, RET_HEADS, 2, S, RET_DK), BF16),
                   jax.ShapeDtypeStruct((B, RET_HEADS, 2, S, RET_DV), BF16),
                   jax.ShapeDtypeStruct((T, wgt.shape[2]), BF16)],
        compiler_params=_cparams(D * n_ret * 2 + 2 * tm * D * 2 + 2 * tm * n_ret * 2
                                 + 8 * PROJ_SUB * PROJ_CHUNK * 4 + 4 * tm * RET_DK * 4, 1),
        name="proj_ret",
    )(h, wqk, wvg, wgt, cosf, sinf, bg)
    n_dil = width(wdil)
    groups = pl.pallas_call(
        _proj_dil_kernel,
        grid=(T // tm,),
        in_specs=[hrow] + [lw(w) for w in wdil] + [full(qn), full(kn)],
        out_specs=[pl.BlockSpec((1, DIL_GROUP_HEADS, 3, d, tm // d, DIL_DH),
                                lambda i: (i // nS, 0, 0, 0, i % nS, 0)) for _, d in DIL_CONFIGS],
        out_shape=[jax.ShapeDtypeStruct((B, DIL_GROUP_HEADS, 3, d, S // d, DIL_DH), BF16)
                   for _, d in DIL_CONFIGS],
        scratch_shapes=[pltpu.VMEM((nh_dil, tm, DIL_DH), F32),
                        pltpu.VMEM((tm // PROJ_SUB, DIL_MID, PROJ_SUB // DIL_MID, DIL_DH), F32)],
        compiler_params=_cparams(D * n_dil * 2 + 2 * tm * D * 2 + 2 * tm * n_dil * 2
                                 + (nh_dil + 1) * tm * DIL_DH * 4 + 8 * PROJ_SUB * PROJ_CHUNK * 4, 1),
        name="proj_dil",
    )(h, *wdil, qn, kn)
    return qk, vg, groups, gates


def _ret_kernel(cdec_ref, qk_ref, vg_ref, dec_ref, ng_ref, o_ref,
                kvf_scr, kvb_scr, fst_scr, bst_scr, *, nchunks):
    C = RET_CHUNK
    q_ref, k_ref = qk_ref.at[0, 0, 0], qk_ref.at[0, 0, 1]
    v_ref, sg_ref = vg_ref.at[0, 0, 0], vg_ref.at[0, 0, 1]
    hd = pl.program_id(1)
    cf = cdec_ref[hd, 0]
    cb = cdec_ref[hd, 1]

    def rows(n):
        return pl.ds(pl.multiple_of(n * C, C), C)

    def kv_body(n, carry):
        kn = k_ref[rows(n), :].astype(F32)
        vn = v_ref[rows(n), :]
        kvf_scr[n] = _dot_tn((kn * dec_ref[0, 3]).astype(BF16), vn)
        kvb_scr[n] = _dot_tn((kn * dec_ref[0, 4]).astype(BF16), vn)
        return carry

    lax.fori_loop(0, nchunks, kv_body, 0, unroll=16)

    st_f = jnp.zeros((RET_DK, RET_DV), F32)
    st_b = jnp.zeros((RET_DK, RET_DV), F32)
    for n in range(nchunks):
        fst_scr[n] = st_f.astype(BF16)
        bst_scr[nchunks - 1 - n] = st_b.astype(BF16)
        if n + 1 < nchunks:
            st_f = cf * st_f + kvf_scr[n]
            st_b = cb * st_b + kvb_scr[nchunks - 1 - n]

    G = RET_BATCH

    def out_body(t, carry):
        rw = pl.ds(pl.multiple_of(t * (G * C), G * C), G * C)
        cs = pl.ds(pl.multiple_of(t * G, G), G)
        q = q_ref[rw, :].reshape(G, C, RET_DK)
        k = k_ref[rw, :].reshape(G, C, RET_DK)
        v = v_ref[rw, :].reshape(G, C, RET_DV)
        q32 = q.astype(F32)
        s = jnp.einsum('gcd,gjd->gcj', q, k, preferred_element_type=F32) * dec_ref[0, 0]
        lhs = jnp.concatenate([s.astype(BF16), (q32 * dec_ref[0, 1]).astype(BF16),
                               (q32 * dec_ref[0, 2]).astype(BF16)], axis=2)
        rhs = jnp.concatenate([v, fst_scr[cs], bst_scr[cs]], axis=1)
        o = jnp.einsum('gck,gke->gce', lhs, rhs, preferred_element_type=F32)
        r = lax.rsqrt(jnp.mean(o * o, axis=-1, keepdims=True) + EPS)
        y = (o * r * ng_ref[0]).reshape(G * C, RET_DV) * sg_ref[rw, :].astype(F32)
        o_ref[0, rw, :] = y.astype(o_ref.dtype)
        return carry

    lax.fori_loop(0, nchunks // G, out_body, 0, unroll=2)


def _retention(qk, vg, cdec, dec, ng):
    B, H, _, S, _ = qk.shape
    nchunks = S // RET_CHUNK
    vmem = (2 * (2 * S * RET_DK * 2 + 2 * S * RET_DV * 2 + S * RET_DV * 2)
            + nchunks * RET_DK * RET_DV * (4 + 4 + 2 + 2) + VMEM_TEMPS)
    return pl.pallas_call(
        functools.partial(_ret_kernel, nchunks=nchunks),
        grid=(B, H),
        in_specs=[pl.BlockSpec(memory_space=pltpu.SMEM),
                  pl.BlockSpec((1, 1, 2, S, RET_DK), lambda b, h: (b, h, 0, 0, 0)),
                  pl.BlockSpec((1, 1, 2, S, RET_DV), lambda b, h: (b, h, 0, 0, 0)),
                  pl.BlockSpec((1, 5, RET_CHUNK, RET_CHUNK), lambda b, h: (h, 0, 0, 0)),
                  pl.BlockSpec((1, 1, RET_DV), lambda b, h: (h, 0, 0))],
        out_specs=pl.BlockSpec((1, S, RET_DV), lambda b, h: (b, 0, h)),
        out_shape=jax.ShapeDtypeStruct((B, S, H * RET_DV), BF16),
        scratch_shapes=[pltpu.VMEM((nchunks, RET_DK, RET_DV), F32),
                        pltpu.VMEM((nchunks, RET_DK, RET_DV), F32),
                        pltpu.VMEM((nchunks, RET_DK, RET_DV), BF16),
                        pltpu.VMEM((nchunks, RET_DK, RET_DV), BF16)],
        compiler_params=_cparams(vmem, 2),
        name="retention",
    )(cdec, qk, vg, dec, ng)


def _dil_kernel(g0_ref, g1_ref, g2_ref, bias_ref, o_ref, osc, lsc, msc, *, seq):
    QB, KW, DH, MID, G = DIL_QBLK, DIL_KWIN, DIL_DH, DIL_MID, DIL_BATCH
    SM = seq // MID
    (q0, k0, v0), (q1, k1, v1), (q2, k2, v2) = (
        tuple(g.at[0, 0, part] for part in range(3)) for g in (g0_ref, g1_ref, g2_ref))
    dils = tuple(d for _, d in DIL_CONFIGS)
    ones = jnp.ones((KW, DH), BF16)

    def window(n, L):
        nblk = L // QB
        if isinstance(n, int):
            i0 = n * QB
            return i0, min(max(i0 - DIL_R, 0), L - KW), (0 if n == 0 else 2 if n == nblk - 1 else 1)
        i0 = pl.multiple_of(n * QB, QB)
        w0 = pl.multiple_of(jnp.clip(i0 - DIL_R, 0, L - KW), DIL_R)
        return i0, w0, jnp.where(n == 0, 0, jnp.where(n == nblk - 1, 2, 1))

    def attend(g, tiles):
        q = jnp.stack([t[0] for t in tiles])
        k = jnp.stack([t[1] for t in tiles])
        v = jnp.stack([jnp.concatenate([t[2], ones], axis=1) for t in tiles])
        s = jnp.einsum('gqd,gkd->gqk', q, k, preferred_element_type=F32)
        s = s + jnp.stack([bias_ref[0, g, t[3]] for t in tiles])
        m = jnp.max(s, axis=-1, keepdims=True)
        p = jnp.exp(s - m).astype(BF16)
        ol = jnp.einsum('gqk,gke->gqe', p, v, preferred_element_type=F32)
        for i, t in enumerate(tiles):
            osc[g, t[4], :] = ol[i, :, :DH]
            lsc[g, t[4], :] = ol[i, :, DH:]
            msc[g, t[4], :] = jnp.broadcast_to(m[i], (QB, DH))

    def g0_body(t, carry):
        tiles = []
        for i in range(G):
            i0, w0, var = window(t * G + i, seq)
            tiles.append((q0[0, pl.ds(i0, QB), :], k0[0, pl.ds(w0, KW), :],
                          v0[0, pl.ds(w0, KW), :], var, pl.ds(i0, QB)))
        attend(0, tiles)
        return carry

    lax.fori_loop(0, seq // (QB * G), g0_body, 0, unroll=2)

    per1 = SM // (QB * G)

    def g1_body(t, carry):
        r = t // per1
        tiles = []
        for i in range(G):
            i0, w0, var = window(i if per1 == 1 else (t % per1) * G + i, SM)
            tiles.append((q1[r, pl.ds(i0, QB), :], k1[r, pl.ds(w0, KW), :],
                          v1[r, pl.ds(w0, KW), :], var,
                          pl.ds(pl.multiple_of(r * SM + i0, QB), QB)))
        attend(1, tiles)
        return carry

    lax.fori_loop(0, MID * per1, g1_body, 0, unroll=2)

    L2 = seq // dils[2]
    pend = []
    for a in range(dils[2] // MID):
        for r in range(MID):
            res = MID * a + r
            for n in range(L2 // QB):
                i0, w0, var = window(n, L2)
                pend.append((q2[res, i0:i0 + QB, :], k2[res, w0:w0 + KW, :],
                             v2[res, w0:w0 + KW, :], var,
                             pl.ds(r * SM + MID * i0 + a, QB, stride=MID)))
                if len(pend) == G:
                    attend(2, pend)
                    pend = []
    assert not pend

    CB = DIL_COMB_ROWS
    for r in range(MID):
        def comb(t, carry, r=r):
            i4 = pl.multiple_of(t * CB, CB)
            rows = (pl.ds(r + MID * i4, CB, stride=MID), pl.ds(r * SM + i4, CB), pl.ds(r * SM + i4, CB))
            ms = [msc[g, rows[g], :] for g in range(3)]
            mx = jnp.maximum(jnp.maximum(ms[0], ms[1]), ms[2])
            es = [jnp.exp(m - mx) for m in ms]
            num = sum(es[g] * osc[g, rows[g], :] for g in range(3))
            den = sum(es[g] * lsc[g, rows[g], :] for g in range(3))
            o_ref[0, 0, pl.ds(i4, CB), r * DH:(r + 1) * DH] = (num / den).astype(o_ref.dtype)
            return carry

        lax.fori_loop(0, SM // CB, comb, 0, unroll=2)


def _dilated_attention(groups, bias):
    B = groups[0].shape[0]
    S = groups[0].shape[3] * groups[0].shape[4]
    H, DH, MID = DIL_GROUP_HEADS, DIL_DH, DIL_MID
    assert [d for _, d in DIL_CONFIGS] == [1, MID, MID * MID] and S % (MID * MID * DIL_KWIN) == 0
    in_specs = [pl.BlockSpec((1, 1) + a.shape[2:], lambda b, j: (b, j, 0, 0, 0, 0)) for a in groups]
    in_specs.append(pl.BlockSpec((1, 3, 3, DIL_QBLK, DIL_KWIN), lambda b, j: (j, 0, 0, 0, 0)))
    args = list(groups) + [bias]
    vmem = (2 * 9 * S * DH * 2 + 2 * 9 * DIL_QBLK * DIL_KWIN * 4 + 2 * S * DH * 2
            + 3 * 3 * S * DH * 4 + VMEM_TEMPS)
    return pl.pallas_call(
        functools.partial(_dil_kernel, seq=S),
        grid=(B, H),
        in_specs=in_specs,
        out_specs=pl.BlockSpec((1, 1, S // MID, MID * DH), lambda b, j: (b, j, 0, 0)),
        out_shape=jax.ShapeDtypeStruct((B, H, S // MID, MID * DH), BF16),
        scratch_shapes=[pltpu.VMEM((3, S, DH), F32)] * 3,
        compiler_params=_cparams(vmem, 2),
        name="dilated_attention",
    )(*args)


def _merge_kernel(x_ref, yr_ref, yd_ref, gt_ref, wro_ref, wdo_ref, wout_ref, gn_ref, xo_ref, ho_ref, yd_scr):
    D = x_ref.shape[1]
    H, MID, DH = yd_scr.shape[0], DIL_MID, DIL_DH
    for r0 in range(0, x_ref.shape[0], MERGE_SUB):
        rows = slice(r0, r0 + MERGE_SUB)
        a = _dot(yr_ref[rows, :], wro_ref[...])
        for j in range(H):
            for r in range(MID):
                yd_scr[j, pl.ds(r0 + r, MERGE_SUB // MID, stride=MID), :] = (
                    yd_ref[0, j, r0 // MID:(r0 + MERGE_SUB) // MID, r * DH:(r + 1) * DH].astype(F32))
        yd = jnp.concatenate([yd_scr[j, rows, :].astype(BF16) for j in range(H)], axis=1)
        b = _dot(yd, wdo_ref[...])
        mg = gt_ref[rows, :D].astype(F32) * a + gt_ref[rows, D:].astype(F32) * b
        xn = x_ref[rows, :] + _dot(mg.astype(BF16), wout_ref[...])
        xo_ref[rows, :] = xn
        ho_ref[rows, :] = _rms(xn, gn_ref[...]).astype(ho_ref.dtype)


def _merge(x, yr, yd, gates, wro, wdo, wout, layer, g_next):
    T, D = x.shape
    tm = MERGE_TILE
    _, H, SM, W = yd.shape
    MID = DIL_MID
    nS = SM * MID // tm
    row = lambda n: pl.BlockSpec((tm, n), lambda i: (i, 0))
    vmem = ((wro[0].size + wdo[0].size + wout[0].size) * 2
            + 2 * tm * (D * 4 + yr.shape[1] * 2 + H * DIL_DH * 2 + gates.shape[1] * 2 + D * 4 + D * 2)
            + H * tm * DIL_DH * 4 + 6 * tm * D * 4)
    return pl.pallas_call(
        _merge_kernel,
        grid=(T // tm,),
        in_specs=[row(D), row(yr.shape[1]),
                  pl.BlockSpec((1, H, tm // MID, W), lambda i: (i // nS, 0, i % nS, 0)), row(gates.shape[1]),
                  _layer_weight(wro, layer), _layer_weight(wdo, layer), _layer_weight(wout, layer),
                  pl.BlockSpec((1, D), lambda i: (0, 0))],
        out_specs=[row(D), row(D)],
        out_shape=[jax.ShapeDtypeStruct((T, D), F32), jax.ShapeDtypeStruct((T, D), BF16)],
        scratch_shapes=[pltpu.VMEM((H, tm, DIL_DH), F32)],
        compiler_params=_cparams(vmem, 1),
        name="merge",
    )(x, yr, yd, gates, wro, wdo, wout, g_next.reshape(1, D))


def _t5_bucket(rel):
    nb = N_BUCKETS // 2
    max_exact = nb // 2
    ret = jnp.where(rel > 0, nb, 0)
    n = jnp.abs(rel)
    nf = jnp.maximum(n, 1).astype(F32)
    large = max_exact + (jnp.log(nf / max_exact) / math.log(MAX_DISTANCE / max_exact)
                         * (nb - max_exact)).astype(jnp.int32)
    large = jnp.minimum(large, nb - 1)
    return ret + jnp.where(n < max_exact, n, large)


def _dil_bias_tiles(rel_bias):
    Q, K, R = DIL_QBLK, DIL_KWIN, DIL_R
    P = Q + K
    per_group = []
    for gi, (_, d) in enumerate(DIL_CONFIGS):
        rb = rel_bias[:, gi * DIL_GROUP_HEADS:(gi + 1) * DIL_GROUP_HEADS].astype(F32)
        tiles = []
        for delta in (0, R, 2 * R):
            off = jnp.arange(P) - (Q - 1) - delta
            bucket = _t5_bucket(jnp.clip(off, -R, R) * d)
            hit = bucket[:, None] == jnp.arange(N_BUCKETS)
            w = jnp.sum(jnp.where(hit[..., None], rb[None], 0.0), axis=1)
            w = jnp.where((jnp.abs(off) <= R)[:, None], w, NEG)
            rows = jnp.tile(w, (Q + 1, 1))[:Q * (P + 1)].reshape(Q, P + 1, -1)
            tiles.append(jnp.flip(rows[:, :K], axis=0))
        per_group.append(jnp.stack(tiles, axis=0))
    t = jnp.stack(per_group, axis=0)
    return jnp.transpose(t, (4, 0, 1, 2, 3))


def _ret_decay_tables(decay_exp):
    C = RET_CHUNK
    lg = jnp.log1p(-jnp.exp2(-decay_exp.astype(F32)))
    lf = lg[0][:, None, None]
    lb = lg[1][:, None, None]
    idx = jnp.arange(C, dtype=F32)
    diff = idx[:, None] - idx[None, :]
    dm = jnp.where(diff[None] >= 0, jnp.exp(lf * jnp.maximum(diff, 0.0)[None]),
                   jnp.exp(lb * jnp.maximum(-diff, 0.0)[None]))
    col = lambda v: jnp.broadcast_to(v[:, :, None], (v.shape[0], C, C))
    qdf = col(jnp.exp(lg[0][:, None] * (idx + 1.0)[None]))
    qdb = col(jnp.exp(lg[1][:, None] * (C - idx)[None]))
    kdf = col(jnp.exp(lg[0][:, None] * (C - 1.0 - idx)[None]))
    kdb = col(jnp.exp(lg[1][:, None] * idx[None]))
    dec = jnp.stack([dm, qdf, qdb, kdf, kdb], axis=1)
    cdec = jnp.stack([jnp.exp(lg[0] * C), jnp.exp(lg[1] * C)], axis=1)
    return dec, cdec


def _rope_tables(S):
    half = RET_DK // 2
    pos = jnp.arange(S, dtype=F32)
    inv = ROPE_BASE ** (-jnp.arange(half, dtype=F32) / half)
    ang = pos[:, None] * inv[None, :]
    cos, sin = jnp.cos(ang), jnp.sin(ang)
    return jnp.concatenate([cos, cos], axis=-1), jnp.concatenate([-sin, sin], axis=-1)


def kernel(x, rel_bias, norm_ffn1, ffn1_gate, ffn1_up, ffn1_down, norm_mix, w_in, b_gate,
           ret_decay_exp, ret_norm, w_ret_o, dil_q_norm, dil_k_norm, w_dil_o, w_out, norm_ffn2,
           ffn2_gate, ffn2_up, ffn2_down):
    B, S, D = x.shape
    depth = w_in.shape[0]
    T = B * S
    assert S % PROJ_TILE == 0 and S % MERGE_TILE == 0
    ret_qk = RET_HEADS * RET_DK
    ret_v = RET_HEADS * RET_DV
    dil_w = DIL_GROUP_HEADS * len(DIL_CONFIGS) * DIL_DH
    o_rq, o_rv = 0, 2 * ret_qk
    o_dq = 2 * ret_qk + 2 * ret_v
    o_dk, o_dv = o_dq + dil_w, o_dq + 2 * dil_w
    o_gate = o_dq + 3 * dil_w
    gw = DIL_GROUP_HEADS * DIL_DH

    cosf, sinf = _rope_tables(S)
    bias_tiles = _dil_bias_tiles(rel_bias)
    bf = lambda a: a.astype(BF16)
    ffn1 = (bf(ffn1_gate), bf(ffn1_up), bf(ffn1_down))
    ffn2 = (bf(ffn2_gate), bf(ffn2_up), bf(ffn2_down))
    wqk, wvg, wgt = bf(w_in[:, :, o_rq:o_rv]), bf(w_in[:, :, o_rv:o_dq]), bf(w_in[:, :, o_gate:])
    wdil = [bf(jnp.concatenate([w_in[:, :, o + gi * gw:o + (gi + 1) * gw] for o in (o_dq, o_dk, o_dv)], axis=2))
            for gi in range(len(DIL_CONFIGS))]
    wro, wdo, wout = bf(w_ret_o), bf(w_dil_o), bf(w_out)

    xf = x.reshape(T, D)
    h = None
    for l in range(depth):
        xf, h = _ffn(xf, h, *ffn1, l, g_in=norm_ffn1[l] if h is None else None, g_next=norm_mix[l])
        qk, vg, groups, gates = _projections(
            h, B, S, wqk, wvg, wdil, wgt, l, cosf, sinf,
            dil_q_norm[l].reshape(1, DIL_DH), dil_k_norm[l].reshape(1, DIL_DH), b_gate[l].reshape(1, 2 * D))

        dec, cdec = _ret_decay_tables(ret_decay_exp[l])
        yr = _retention(qk, vg, cdec, dec, ret_norm[l].reshape(RET_HEADS, 1, RET_DV))
        yd = _dilated_attention(groups, bias_tiles)

        xf, h = _merge(xf, yr.reshape(T, ret_v), yd, gates, wro, wdo, wout, l, norm_ffn2[l])
        if l + 1 < depth:
            xf, h = _ffn(xf, h, *ffn2, l, g_next=norm_ffn1[l + 1])
        else:
            xf, = _ffn(xf, h, *ffn2, l)
    return xf.reshape(B, S, D)
```

```python
import functools
import math

import jax
import jax.numpy as jnp
from jax import lax
from jax.experimental import pallas as pl
from jax.experimental.pallas import tpu as pltpu

F32 = jnp.float32
BF16 = jnp.bfloat16

V7X_VMEM_BYTES = 64 * 1024 * 1024
VMEM_LIMIT_CAP = V7X_VMEM_BYTES - (4 << 20)
VMEM_TEMPS = 8 << 20

EPS = 1e-6
ROPE_BASE = 10000.0
NEG = -1e30

RET_HEADS = 4
RET_DK = 128
RET_DV = 256
RET_CHUNK = 128
RET_BATCH = 8

DIL_CONFIGS = ((128, 1), (512, 4), (2048, 16))
DIL_GROUP_HEADS = 4
DIL_DH = 128
DIL_R = 64
DIL_QBLK = 2 * DIL_R
DIL_KWIN = 4 * DIL_R
DIL_BATCH = 8
DIL_MID = 4
DIL_COMB_ROWS = 256
N_BUCKETS = 32
MAX_DISTANCE = 1024

FFN_TILE = 1024
FFN_SUB = 512
PROJ_TILE = 1024
PROJ_SUB = 512
MERGE_TILE = 1024
MERGE_SUB = 512
FFN_CHUNK = 512
PROJ_CHUNK = 256


def _cparams(vmem_bytes, ngrid):
    return pltpu.CompilerParams(
        dimension_semantics=("arbitrary",) * ngrid,
        vmem_limit_bytes=int(min(vmem_bytes, VMEM_LIMIT_CAP)),
    )


def _layer_weight(w_all, layer):
    zeros = (0,) * (w_all.ndim - 1)
    return pl.BlockSpec((None,) + w_all.shape[1:], lambda *_: (layer,) + zeros,
                        pipeline_mode=pl.Buffered(1))


def _dot(a, b):
    return jnp.dot(a, b, preferred_element_type=F32)


def _dot_nt(a, b):
    return lax.dot_general(a, b, (((1,), (1,)), ((), ())), preferred_element_type=F32)


def _dot_tn(a, b):
    return lax.dot_general(a, b, (((0,), (0,)), ((), ())), preferred_element_type=F32)


def _rms(x32, g):
    ms = jnp.mean(x32 * x32, axis=-1, keepdims=True)
    return x32 * lax.rsqrt(ms + EPS) * g


def _sigmoid(x):
    return 0.5 * jnp.tanh(0.5 * x) + 0.5


def _ffn_kernel(x_ref, hg_ref, wg_ref, wu_ref, wd_ref, *rest, chunks, norm_in):
    for r0 in range(0, x_ref.shape[0], FFN_SUB):
        rows = slice(r0, r0 + FFN_SUB)
        h = _rms(x_ref[rows, :], hg_ref[...]).astype(BF16) if norm_in else hg_ref[rows, :]
        acc = None
        for c0, cw in chunks:
            g = _dot(h, wg_ref[:, c0:c0 + cw])
            u = _dot(h, wu_ref[:, c0:c0 + cw])
            a = (g * _sigmoid(g) * u).astype(BF16)
            d = _dot(a, wd_ref[c0:c0 + cw, :])
            acc = d if acc is None else acc + d
        xn = x_ref[rows, :] + 0.5 * acc
        if len(rest) == 3:
            gn_ref, xo_ref, ho_ref = rest
            ho_ref[rows, :] = _rms(xn, gn_ref[...]).astype(ho_ref.dtype)
        else:
            xo_ref, = rest
        xo_ref[rows, :] = xn


def _ffn(x, h, wg, wu, wd, layer, g_in=None, g_next=None):
    assert (h is None) == (g_in is not None)
    T, D = x.shape
    Fh = wg.shape[2]
    tm = FFN_TILE
    chunks = tuple((c0, min(FFN_CHUNK, Fh - c0)) for c0 in range(0, Fh, FFN_CHUNK))
    vmem = 3 * D * Fh * 2 + 2 * tm * D * (4 + 2 + 4 + 2) + 4 * tm * D * 4
    row = pl.BlockSpec((tm, D), lambda i: (i, 0))
    vec = pl.BlockSpec((1, D), lambda i: (0, 0))
    in_specs = [row, vec if h is None else row,
                _layer_weight(wg, layer), _layer_weight(wu, layer), _layer_weight(wd, layer)]
    args = [x, g_in.reshape(1, D) if h is None else h, wg, wu, wd]
    out_specs = [row]
    out_shape = [jax.ShapeDtypeStruct((T, D), F32)]
    if g_next is not None:
        in_specs.append(vec)
        args.append(g_next.reshape(1, D))
        out_specs.append(row)
        out_shape.append(jax.ShapeDtypeStruct((T, D), BF16))
    return pl.pallas_call(
        functools.partial(_ffn_kernel, chunks=chunks, norm_in=h is None),
        grid=(T // tm,),
        in_specs=in_specs,
        out_specs=out_specs,
        out_shape=out_shape,
        compiler_params=_cparams(vmem, 1),
        name="ffn",
    )(*args)


def _retqk_body(h_ref, w_ref, cos_ref, sin_ref, o_ref, r0, n):
    rows = slice(r0, r0 + n)
    cos = cos_ref[rows, :]
    sin = sin_ref[rows, :]
    per = PROJ_CHUNK // RET_DK
    for c in range(w_ref.shape[1] // PROJ_CHUNK):
        res = _dot(h_ref[rows, :], w_ref[:, c * PROJ_CHUNK:(c + 1) * PROJ_CHUNK])
        for k in range(per):
            t = res[:, k * RET_DK:(k + 1) * RET_DK]
            r = t * cos + pltpu.roll(t, RET_DK // 2, axis=1) * sin
            hd = c * per + k
            if hd >= RET_HEADS:
                r = r * (RET_DK ** -0.5)
            o_ref[0, hd % RET_HEADS, hd // RET_HEADS, rows, :] = r.astype(o_ref.dtype)


def _retvg_body(h_ref, w_ref, o_ref, r0, n):
    rows = slice(r0, r0 + n)
    per = PROJ_CHUNK // RET_DV
    for c in range(w_ref.shape[1] // PROJ_CHUNK):
        res = _dot(h_ref[rows, :], w_ref[:, c * PROJ_CHUNK:(c + 1) * PROJ_CHUNK])
        for k in range(per):
            t = res[:, k * RET_DV:(k + 1) * RET_DV]
            hd = c * per + k
            if hd >= RET_HEADS:
                t = t * _sigmoid(t)
            o_ref[0, hd % RET_HEADS, hd // RET_HEADS, rows, :] = t.astype(o_ref.dtype)


def _gate_body(h_ref, w_ref, b_ref, o_ref, r0, n):
    rows = slice(r0, r0 + n)
    for c0 in range(0, w_ref.shape[1], PROJ_CHUNK):
        res = _dot(h_ref[rows, :], w_ref[:, c0:c0 + PROJ_CHUNK]) + b_ref[:, c0:c0 + PROJ_CHUNK]
        o_ref[rows, c0:c0 + PROJ_CHUNK] = _sigmoid(res).astype(o_ref.dtype)


def _dil_body(h_ref, w_ref, qn, kn, o_ref, scr_ref, scr2_ref, d, r0, n):
    rows = slice(r0, r0 + n)
    MID = DIL_MID
    per = PROJ_CHUNK // DIL_DH
    for c in range(w_ref.shape[1] // PROJ_CHUNK):
        res = _dot(h_ref[rows, :], w_ref[:, c * PROJ_CHUNK:(c + 1) * PROJ_CHUNK])
        for k in range(per):
            t = res[:, k * DIL_DH:(k + 1) * DIL_DH]
            hd = c * per + k
            if hd < DIL_GROUP_HEADS:
                t = _rms(t, qn)
            elif hd < 2 * DIL_GROUP_HEADS:
                t = _rms(t, kn)
            if d == 1:
                o_ref[0, hd % DIL_GROUP_HEADS, hd // DIL_GROUP_HEADS, 0, rows, :] = t.astype(o_ref.dtype)
            else:
                scr_ref[hd, rows, :] = t
    if d == 1:
        return
    for hd in range(3 * DIL_GROUP_HEADS):
        out = o_ref.at[0, hd % DIL_GROUP_HEADS, hd // DIL_GROUP_HEADS]
        orow = slice(r0 // d, (r0 + n) // d)
        if d == MID:
            for r in range(d):
                out[r, orow, :] = scr_ref[hd, pl.ds(r0 + r, n // d, stride=d), :].astype(o_ref.dtype)
        else:
            assert d == MID * MID
            for r in range(MID):
                scr2_ref[r] = scr_ref[hd, pl.ds(r0 + r, n // MID, stride=MID), :]
            for r in range(MID):
                for a in range(MID):
                    out[MID * a + r, orow, :] = (
                        scr2_ref[r, pl.ds(a, n // d, stride=MID), :].astype(o_ref.dtype))


def _proj_ret_kernel(h_ref, wqk_ref, wvg_ref, wgt_ref, cos_ref, sin_ref, bg_ref, oqk_ref, ovg_ref, ogt_ref):
    for r0 in range(0, h_ref.shape[0], PROJ_SUB):
        _retqk_body(h_ref, wqk_ref, cos_ref, sin_ref, oqk_ref, r0, PROJ_SUB)
        _retvg_body(h_ref, wvg_ref, ovg_ref, r0, PROJ_SUB)
        _gate_body(h_ref, wgt_ref, bg_ref, ogt_ref, r0, PROJ_SUB)


def _proj_dil_kernel(h_ref, wd0_ref, wd1_ref, wd2_ref, qn_ref, kn_ref, od0_ref, od1_ref, od2_ref,
                     scr_ref, scr2_ref):
    qn = qn_ref[...] * (DIL_DH ** -0.5)
    kn = kn_ref[...]
    for i, r0 in enumerate(range(0, h_ref.shape[0], PROJ_SUB)):
        for w_ref, o_ref, (_, d) in zip((wd0_ref, wd1_ref, wd2_ref), (od0_ref, od1_ref, od2_ref), DIL_CONFIGS):
            _dil_body(h_ref, w_ref, qn, kn, o_ref, scr_ref, scr2_ref.at[i], d, r0, PROJ_SUB)


def _projections(h, B, S, wqk, wvg, wdil, wgt, layer, cosf, sinf, qn, kn, bg):
    T, D = h.shape
    tm = PROJ_TILE
    nS = S // tm
    nh_dil = 3 * DIL_GROUP_HEADS
    full = lambda a: pl.BlockSpec(a.shape, lambda i: (0,) * a.ndim)
    lw = lambda w: _layer_weight(w, layer)
    hrow = pl.BlockSpec((tm, D), lambda i: (i, 0))
    seq_idx = lambda i: (i // nS, 0, 0, i % nS, 0)
    width = lambda ws: sum(w.shape[2] for w in ws)
    n_ret = width([wqk, wvg, wgt])
    qk, vg, gates = pl.pallas_call(
        _proj_ret_kernel,
        grid=(T // tm,),
        in_specs=[hrow, lw(wqk), lw(wvg), lw(wgt),
                  pl.BlockSpec((tm, RET_DK), lambda i: (i % nS, 0)),
                  pl.BlockSpec((tm, RET_DK), lambda i: (i % nS, 0)), full(bg)],
        out_specs=[pl.BlockSpec((1, RET_HEADS, 2, tm, RET_DK), seq_idx),
                   pl.BlockSpec((1, RET_HEADS, 2, tm, RET_DV), seq_idx),
                   pl.BlockSpec((tm, wgt.shape[2]), lambda i: (i, 0))],
        out_shape=[jax.ShapeDtypeStruct((B, RET_HEADS, 2, S, RET_DK), BF16),
                   jax.ShapeDtypeStruct((B, RET_HEADS, 2, S, RET_DV), BF16),
                   jax.ShapeDtypeStruct((T, wgt.shape[2]), BF16)],
        compiler_params=_cparams(D * n_ret * 2 + 2 * tm * D * 2 + 2 * tm * n_ret * 2
                                 + 8 * PROJ_SUB * PROJ_CHUNK * 4 + 4 * tm * RET_DK * 4, 1),
        name="proj_ret",
    )(h, wqk, wvg, wgt, cosf, sinf, bg)
    n_dil = width(wdil)
    groups = pl.pallas_call(
        _proj_dil_kernel,
        grid=(T // tm,),
        in_specs=[hrow] + [lw(w) for w in wdil] + [full(qn), full(kn)],
        out_specs=[pl.BlockSpec((1, DIL_GROUP_HEADS, 3, d, tm // d, DIL_DH),
                                lambda i: (i // nS, 0, 0, 0, i % nS, 0)) for _, d in DIL_CONFIGS],
        out_shape=[jax.ShapeDtypeStruct((B, DIL_GROUP_HEADS, 3, d, S // d, DIL_DH), BF16)
                   for _, d in DIL_CONFIGS],
        scratch_shapes=[pltpu.VMEM((nh_dil, tm, DIL_DH), F32),
                        pltpu.VMEM((tm // PROJ_SUB, DIL_MID, PROJ_SUB // DIL_MID, DIL_DH), F32)],
        compiler_params=_cparams(D * n_dil * 2 + 2 * tm * D * 2 + 2 * tm * n_dil * 2
                                 + (nh_dil + 1) * tm * DIL_DH * 4 + 8 * PROJ_SUB * PROJ_CHUNK * 4, 1),
        name="proj_dil",
    )(h, *wdil, qn, kn)
    return qk, vg, groups, gates


def _ret_kernel(cdec_ref, qk_ref, vg_ref, dec_ref, ng_ref, o_ref,
                kvf_scr, kvb_scr, fst_scr, bst_scr, *, nchunks):
    C = RET_CHUNK
    q_ref, k_ref = qk_ref.at[0, 0, 0], qk_ref.at[0, 0, 1]
    v_ref, sg_ref = vg_ref.at[0, 0, 0], vg_ref.at[0, 0, 1]
    hd = pl.program_id(1)
    cf = cdec_ref[hd, 0]
    cb = cdec_ref[hd, 1]

    def rows(n):
        return pl.ds(pl.multiple_of(n * C, C), C)

    def kv_body(n, carry):
        kn = k_ref[rows(n), :].astype(F32)
        vn = v_ref[rows(n), :]
        kvf_scr[n] = _dot_tn((kn * dec_ref[0, 3]).astype(BF16), vn)
        kvb_scr[n] = _dot_tn((kn * dec_ref[0, 4]).astype(BF16), vn)
        return carry

    lax.fori_loop(0, nchunks, kv_body, 0, unroll=32)

    st_f = jnp.zeros((RET_DK, RET_DV), F32)
    st_b = jnp.zeros((RET_DK, RET_DV), F32)
    for n in range(nchunks):
        fst_scr[n] = st_f.astype(BF16)
        bst_scr[nchunks - 1 - n] = st_b.astype(BF16)
        if n + 1 < nchunks:
            st_f = cf * st_f + kvf_scr[n]
            st_b = cb * st_b + kvb_scr[nchunks - 1 - n]

    G = RET_BATCH

    def out_body(t, carry):
        rw = pl.ds(pl.multiple_of(t * (G * C), G * C), G * C)
        cs = pl.ds(pl.multiple_of(t * G, G), G)
        q = q_ref[rw, :].reshape(G, C, RET_DK)
        k = k_ref[rw, :].reshape(G, C, RET_DK)
        v = v_ref[rw, :].reshape(G, C, RET_DV)
        q32 = q.astype(F32)
        s = jnp.einsum('gcd,gjd->gcj', q, k, preferred_element_type=F32) * dec_ref[0, 0]
        lhs = jnp.concatenate([s.astype(BF16), (q32 * dec_ref[0, 1]).astype(BF16),
                               (q32 * dec_ref[0, 2]).astype(BF16)], axis=2)
        rhs = jnp.concatenate([v, fst_scr[cs], bst_scr[cs]], axis=1)
        o = jnp.einsum('gck,gke->gce', lhs, rhs, preferred_element_type=F32)
        r = lax.rsqrt(jnp.mean(o * o, axis=-1, keepdims=True) + EPS)
        y = (o * r * ng_ref[0]).reshape(G * C, RET_DV) * sg_ref[rw, :].astype(F32)
        o_ref[0, rw, :] = y.astype(o_ref.dtype)
        return carry

    lax.fori_loop(0, nchunks // G, out_body, 0, unroll=4)


def _retention(qk, vg, cdec, dec, ng):
    B, H, _, S, _ = qk.shape
    nchunks = S // RET_CHUNK
    vmem = (2 * (2 * S * RET_DK * 2 + 2 * S * RET_DV * 2 + S * RET_DV * 2)
            + nchunks * RET_DK * RET_DV * (4 + 4 + 2 + 2) + VMEM_TEMPS)
    return pl.pallas_call(
        functools.partial(_ret_kernel, nchunks=nchunks),
        grid=(B, H),
        in_specs=[pl.BlockSpec(memory_space=pltpu.SMEM),
                  pl.BlockSpec((1, 1, 2, S, RET_DK), lambda b, h: (b, h, 0, 0, 0)),
                  pl.BlockSpec((1, 1, 2, S, RET_DV), lambda b, h: (b, h, 0, 0, 0)),
                  pl.BlockSpec((1, 5, RET_CHUNK, RET_CHUNK), lambda b, h: (h, 0, 0, 0)),
                  pl.BlockSpec((1, 1, RET_DV), lambda b, h: (h, 0, 0))],
        out_specs=pl.BlockSpec((1, S, RET_DV), lambda b, h: (b, 0, h)),
        out_shape=jax.ShapeDtypeStruct((B, S, H * RET_DV), BF16),
        scratch_shapes=[pltpu.VMEM((nchunks, RET_DK, RET_DV), F32),
                        pltpu.VMEM((nchunks, RET_DK, RET_DV), F32),
                        pltpu.VMEM((nchunks, RET_DK, RET_DV), BF16),
                        pltpu.VMEM((nchunks, RET_DK, RET_DV), BF16)],
        compiler_params=_cparams(vmem, 2),
        name="retention",
    )(cdec, qk, vg, dec, ng)


def _dil_kernel(g0_ref, g1_ref, g2_ref, bias_ref, o_ref, osc, lsc, msc, *, seq):
    QB, KW, DH, MID, G = DIL_QBLK, DIL_KWIN, DIL_DH, DIL_MID, DIL_BATCH
    SM = seq // MID
    (q0, k0, v0), (q1, k1, v1), (q2, k2, v2) = (
        tuple(g.at[0, 0, part] for part in range(3)) for g in (g0_ref, g1_ref, g2_ref))
    dils = tuple(d for _, d in DIL_CONFIGS)
    ones = jnp.ones((KW, DH), BF16)

    def window(n, L):
        nblk = L // QB
        if isinstance(n, int):
            i0 = n * QB
            return i0, min(max(i0 - DIL_R, 0), L - KW), (0 if n == 0 else 2 if n == nblk - 1 else 1)
        i0 = pl.multiple_of(n * QB, QB)
        w0 = pl.multiple_of(jnp.clip(i0 - DIL_R, 0, L - KW), DIL_R)
        return i0, w0, jnp.where(n == 0, 0, jnp.where(n == nblk - 1, 2, 1))

    def attend(g, tiles):
        q = jnp.stack([t[0] for t in tiles])
        k = jnp.stack([t[1] for t in tiles])
        v = jnp.stack([jnp.concatenate([t[2], ones], axis=1) for t in tiles])
        s = jnp.einsum('gqd,gkd->gqk', q, k, preferred_element_type=F32)
        s = s + jnp.stack([bias_ref[0, g, t[3]] for t in tiles])
        m = jnp.max(s, axis=-1, keepdims=True)
        p = jnp.exp(s - m).astype(BF16)
        ol = jnp.einsum('gqk,gke->gqe', p, v, preferred_element_type=F32)
        for i, t in enumerate(tiles):
            osc[g, t[4], :] = ol[i, :, :DH]
            lsc[g, t[4], :] = ol[i, :, DH:]
            msc[g, t[4], :] = jnp.broadcast_to(m[i], (QB, DH))

    def g0_body(t, carry):
        tiles = []
        for i in range(G):
            i0, w0, var = window(t * G + i, seq)
            tiles.append((q0[0, pl.ds(i0, QB), :], k0[0, pl.ds(w0, KW), :],
                          v0[0, pl.ds(w0, KW), :], var, pl.ds(i0, QB)))
        attend(0, tiles)
        return carry

    lax.fori_loop(0, seq // (QB * G), g0_body, 0, unroll=4)

    per1 = SM // (QB * G)

    def g1_body(t, carry):
        r = t // per1
        tiles = []
        for i in range(G):
            i0, w0, var = window(i if per1 == 1 else (t % per1) * G + i, SM)
            tiles.append((q1[r, pl.ds(i0, QB), :], k1[r, pl.ds(w0, KW), :],
                          v1[r, pl.ds(w0, KW), :], var,
                          pl.ds(pl.multiple_of(r * SM + i0, QB), QB)))
        attend(1, tiles)
        return carry

    lax.fori_loop(0, MID * per1, g1_body, 0, unroll=4)

    L2 = seq // dils[2]
    pend = []
    for a in range(dils[2] // MID):
        for r in range(MID):
            res = MID * a + r
            for n in range(L2 // QB):
                i0, w0, var = window(n, L2)
                pend.append((q2[res, i0:i0 + QB, :], k2[res, w0:w0 + KW, :],
                             v2[res, w0:w0 + KW, :], var,
                             pl.ds(r * SM + MID * i0 + a, QB, stride=MID)))
                if len(pend) == G:
                    attend(2, pend)
                    pend = []
    assert not pend

    CB = DIL_COMB_ROWS
    for r in range(MID):
        def comb(t, carry, r=r):
            i4 = pl.multiple_of(t * CB, CB)
            rows = (pl.ds(r + MID * i4, CB, stride=MID), pl.ds(r * SM + i4, CB), pl.ds(r * SM + i4, CB))
            ms = [msc[g, rows[g], :] for g in range(3)]
            mx = jnp.maximum(jnp.maximum(ms[0], ms[1]), ms[2])
            es = [jnp.exp(m - mx) for m in ms]
            num = sum(es[g] * osc[g, rows[g], :] for g in range(3))
            den = sum(es[g] * lsc[g, rows[g], :] for g in range(3))
            o_ref[0, 0, pl.ds(i4, CB), r * DH:(r + 1) * DH] = (num / den).astype(o_ref.dtype)
            return carry

        lax.fori_loop(0, SM // CB, comb, 0, unroll=2)


def _dilated_attention(groups, bias):
    B = groups[0].shape[0]
    S = groups[0].shape[3] * groups[0].shape[4]
    H, DH, MID = DIL_GROUP_HEADS, DIL_DH, DIL_MID
    assert [d for _, d in DIL_CONFIGS] == [1, MID, MID * MID] and S % (MID * MID * DIL_KWIN) == 0
    in_specs = [pl.BlockSpec((1, 1) + a.shape[2:], lambda b, j: (b, j, 0, 0, 0, 0)) for a in groups]
    in_specs.append(pl.BlockSpec((1, 3, 3, DIL_QBLK, DIL_KWIN), lambda b, j: (j, 0, 0, 0, 0)))
    args = list(groups) + [bias]
    vmem = (2 * 9 * S * DH * 2 + 2 * 9 * DIL_QBLK * DIL_KWIN * 4 + 2 * S * DH * 2
            + 3 * 3 * S * DH * 4 + VMEM_TEMPS)
    return pl.pallas_call(
        functools.partial(_dil_kernel, seq=S),
        grid=(B, H),
        in_specs=in_specs,
        out_specs=pl.BlockSpec((1, 1, S // MID, MID * DH), lambda b, j: (b, j, 0, 0)),
        out_shape=jax.ShapeDtypeStruct((B, H, S // MID, MID * DH), BF16),
        scratch_shapes=[pltpu.VMEM((3, S, DH), F32)] * 3,
        compiler_params=_cparams(vmem, 2),
        name="dilated_attention",
    )(*args)


def _merge_kernel(x_ref, yr_ref, yd_ref, gt_ref, wro_ref, wdo_ref, wout_ref, gn_ref, xo_ref, ho_ref, yd_scr):
    D = x_ref.shape[1]
    H, MID, DH = yd_scr.shape[0], DIL_MID, DIL_DH
    for r0 in range(0, x_ref.shape[0], MERGE_SUB):
        rows = slice(r0, r0 + MERGE_SUB)
        a = _dot(yr_ref[rows, :], wro_ref[...])
        for j in range(H):
            for r in range(MID):
                yd_scr[j, pl.ds(r0 + r, MERGE_SUB // MID, stride=MID), :] = (
                    yd_ref[0, j, r0 // MID:(r0 + MERGE_SUB) // MID, r * DH:(r + 1) * DH].astype(F32))
        yd = jnp.concatenate([yd_scr[j, rows, :].astype(BF16) for j in range(H)], axis=1)
        b = _dot(yd, wdo_ref[...])
        mg = gt_ref[rows, :D].astype(F32) * a + gt_ref[rows, D:].astype(F32) * b
        xn = x_ref[rows, :] + _dot(mg.astype(BF16), wout_ref[...])
        xo_ref[rows, :] = xn
        ho_ref[rows, :] = _rms(xn, gn_ref[...]).astype(ho_ref.dtype)


def _merge(x, yr, yd, gates, wro, wdo, wout, layer, g_next):
    T, D = x.shape
    tm = MERGE_TILE
    _, H, SM, W = yd.shape
    MID = DIL_MID
    nS = SM * MID // tm
    row = lambda n: pl.BlockSpec((tm, n), lambda i: (i, 0))
    vmem = ((wro[0].size + wdo[0].size + wout[0].size) * 2
            + 2 * tm * (D * 4 + yr.shape[1] * 2 + H * DIL_DH * 2 + gates.shape[1] * 2 + D * 4 + D * 2)
            + H * tm * DIL_DH * 4 + 6 * tm * D * 4)
    return pl.pallas_call(
        _merge_kernel,
        grid=(T // tm,),
        in_specs=[row(D), row(yr.shape[1]),
                  pl.BlockSpec((1, H, tm // MID, W), lambda i: (i // nS, 0, i % nS, 0)), row(gates.shape[1]),
                  _layer_weight(wro, layer), _layer_weight(wdo, layer), _layer_weight(wout, layer),
                  pl.BlockSpec((1, D), lambda i: (0, 0))],
        out_specs=[row(D), row(D)],
        out_shape=[jax.ShapeDtypeStruct((T, D), F32), jax.ShapeDtypeStruct((T, D), BF16)],
        scratch_shapes=[pltpu.VMEM((H, tm, DIL_DH), F32)],
        compiler_params=_cparams(vmem, 1),
        name="merge",
    )(x, yr, yd, gates, wro, wdo, wout, g_next.reshape(1, D))


def _t5_bucket(rel):
    nb = N_BUCKETS // 2
    max_exact = nb // 2
    ret = jnp.where(rel > 0, nb, 0)
    n = jnp.abs(rel)
    nf = jnp.maximum(n, 1).astype(F32)
    large = max_exact + (jnp.log(nf / max_exact) / math.log(MAX_DISTANCE / max_exact)
                         * (nb - max_exact)).astype(jnp.int32)
    large = jnp.minimum(large, nb - 1)
    return ret + jnp.where(n < max_exact, n, large)


def _dil_bias_tiles(rel_bias):
    Q, K, R = DIL_QBLK, DIL_KWIN, DIL_R
    P = Q + K
    per_group = []
    for gi, (_, d) in enumerate(DIL_CONFIGS):
        rb = rel_bias[:, gi * DIL_GROUP_HEADS:(gi + 1) * DIL_GROUP_HEADS].astype(F32)
        tiles = []
        for delta in (0, R, 2 * R):
            off = jnp.arange(P) - (Q - 1) - delta
            bucket = _t5_bucket(jnp.clip(off, -R, R) * d)
            hit = bucket[:, None] == jnp.arange(N_BUCKETS)
            w = jnp.sum(jnp.where(hit[..., None], rb[None], 0.0), axis=1)
            w = jnp.where((jnp.abs(off) <= R)[:, None], w, NEG)
            rows = jnp.tile(w, (Q + 1, 1))[:Q * (P + 1)].reshape(Q, P + 1, -1)
            tiles.append(jnp.flip(rows[:, :K], axis=0))
        per_group.append(jnp.stack(tiles, axis=0))
    t = jnp.stack(per_group, axis=0)
    return jnp.transpose(t, (4, 0, 1, 2, 3))


def _ret_decay_tables(decay_exp):
    C = RET_CHUNK
    lg = jnp.log1p(-jnp.exp2(-decay_exp.astype(F32)))
    lf = lg[0][:, None, None]
    lb = lg[1][:, None, None]
    idx = jnp.arange(C, dtype=F32)
    diff = idx[:, None] - idx[None, :]
    dm = jnp.where(diff[None] >= 0, jnp.exp(lf * jnp.maximum(diff, 0.0)[None]),
                   jnp.exp(lb * jnp.maximum(-diff, 0.0)[None]))
    col = lambda v: jnp.broadcast_to(v[:, :, None], (v.shape[0], C, C))
    qdf = col(jnp.exp(lg[0][:, None] * (idx + 1.0)[None]))
    qdb = col(jnp.exp(lg[1][:, None] * (C - idx)[None]))
    kdf = col(jnp.exp(lg[0][:, None] * (C - 1.0 - idx)[None]))
    kdb = col(jnp.exp(lg[1][:, None] * idx[None]))
    dec = jnp.stack([dm, qdf, qdb, kdf, kdb], axis=1)
    cdec = jnp.stack([jnp.exp(lg[0] * C), jnp.exp(lg[1] * C)], axis=1)
    return dec, cdec


def _rope_tables(S):
    half = RET_DK // 2
    pos = jnp.arange(S, dtype=F32)
    inv = ROPE_BASE ** (-jnp.arange(half, dtype=F32) / half)
    ang = pos[:, None] * inv[None, :]
    cos, sin = jnp.cos(ang), jnp.sin(ang)
    return jnp.concatenate([cos, cos], axis=-1), jnp.concatenate([-sin, sin], axis=-1)


def kernel(x, rel_bias, norm_ffn1, ffn1_gate, ffn1_up, ffn1_down, norm_mix, w_in, b_gate,
           ret_decay_exp, ret_norm, w_ret_o, dil_q_norm, dil_k_norm, w_dil_o, w_out, norm_ffn2,
           ffn2_gate, ffn2_up, ffn2_down):
    B, S, D = x.shape
    depth = w_in.shape[0]
    T = B * S
    assert S % PROJ_TILE == 0 and S % MERGE_TILE == 0
    ret_qk = RET_HEADS * RET_DK
    ret_v = RET_HEADS * RET_DV
    dil_w = DIL_GROUP_HEADS * len(DIL_CONFIGS) * DIL_DH
    o_rq, o_rv = 0, 2 * ret_qk
    o_dq = 2 * ret_qk + 2 * ret_v
    o_dk, o_dv = o_dq + dil_w, o_dq + 2 * dil_w
    o_gate = o_dq + 3 * dil_w
    gw = DIL_GROUP_HEADS * DIL_DH

    cosf, sinf = _rope_tables(S)
    bias_tiles = _dil_bias_tiles(rel_bias)
    bf = lambda a: a.astype(BF16)
    ffn1 = (bf(ffn1_gate), bf(ffn1_up), bf(ffn1_down))
    ffn2 = (bf(ffn2_gate), bf(ffn2_up), bf(ffn2_down))
    wqk, wvg, wgt = bf(w_in[:, :, o_rq:o_rv]), bf(w_in[:, :, o_rv:o_dq]), bf(w_in[:, :, o_gate:])
    wdil = [bf(jnp.concatenate([w_in[:, :, o + gi * gw:o + (gi + 1) * gw] for o in (o_dq, o_dk, o_dv)], axis=2))
            for gi in range(len(DIL_CONFIGS))]
    wro, wdo, wout = bf(w_ret_o), bf(w_dil_o), bf(w_out)

    xf = x.reshape(T, D)
    h = None
    for l in range(depth):
        xf, h = _ffn(xf, h, *ffn1, l, g_in=norm_ffn1[l] if h is None else None, g_next=norm_mix[l])
        qk, vg, groups, gates = _projections(
            h, B, S, wqk, wvg, wdil, wgt, l, cosf, sinf,
            dil_q_norm[l].reshape(1, DIL_DH), dil_k_norm[l].reshape(1, DIL_DH), b_gate[l].reshape(1, 2 * D))

        dec, cdec = _ret_decay_tables(ret_decay_exp[l])
        yr = _retention(qk, vg, cdec, dec, ret_norm[l].reshape(RET_HEADS, 1, RET_DV))
        yd = _dilated_attention(groups, bias_tiles)

        xf, h = _merge(xf, yr.reshape(T, ret_v), yd, gates, wro, wdo, wout, l, norm_ffn2[l])
        if l + 1 < depth:
            xf, h = _ffn(xf, h, *ffn2, l, g_next=norm_ffn1[l + 1])
        else:
            xf, = _ffn(xf, h, *ffn2, l)
    return xf.reshape(B, S, D)
```

```python
import functools
import math

import jax
import jax.numpy as jnp
from jax import lax
from jax.experimental import pallas as pl
from jax.experimental.pallas import tpu as pltpu

F32 = jnp.float32
BF16 = jnp.bfloat16

V7X_VMEM_BYTES = 64 * 1024 * 1024
VMEM_LIMIT_CAP = V7X_VMEM_BYTES - (4 << 20)
VMEM_TEMPS = 8 << 20

EPS = 1e-6
ROPE_BASE = 10000.0
NEG = -1e30

RET_HEADS = 4
RET_DK = 128
RET_DV = 256
RET_CHUNK = 128
RET_BATCH = 8

DIL_CONFIGS = ((128, 1), (512, 4), (2048, 16))
DIL_GROUP_HEADS = 4
DIL_DH = 128
DIL_R = 64
DIL_QBLK = 2 * DIL_R
DIL_KWIN = 4 * DIL_R
DIL_BATCH = 8
DIL_MID = 4
DIL_COMB_ROWS = 256
N_BUCKETS = 32
MAX_DISTANCE = 1024

FFN_TILE = 1024
FFN_SUB = 512
PROJ_TILE = 1024
PROJ_SUB = 512
MERGE_TILE = 1024
MERGE_SUB = 512
FFN_CHUNK = 512
PROJ_CHUNK = 256


def _cparams(vmem_bytes, ngrid):
    return pltpu.CompilerParams(
        dimension_semantics=("arbitrary",) * ngrid,
        vmem_limit_bytes=int(min(vmem_bytes, VMEM_LIMIT_CAP)),
    )


def _layer_weight(w_all, layer):
    zeros = (0,) * (w_all.ndim - 1)
    return pl.BlockSpec((None,) + w_all.shape[1:], lambda *_: (layer,) + zeros,
                        pipeline_mode=pl.Buffered(1))


def _dot(a, b):
    return jnp.dot(a, b, preferred_element_type=F32)


def _dot_nt(a, b):
    return lax.dot_general(a, b, (((1,), (1,)), ((), ())), preferred_element_type=F32)


def _dot_tn(a, b):
    return lax.dot_general(a, b, (((0,), (0,)), ((), ())), preferred_element_type=F32)


def _rms(x32, g):
    ms = jnp.mean(x32 * x32, axis=-1, keepdims=True)
    return x32 * lax.rsqrt(ms + EPS) * g


def _sigmoid(x):
    return 0.5 * jnp.tanh(0.5 * x) + 0.5


def _ffn_kernel(x_ref, hg_ref, wg_ref, wu_ref, wd_ref, *rest, chunks, norm_in):
    for r0 in range(0, x_ref.shape[0], FFN_SUB):
        rows = slice(r0, r0 + FFN_SUB)
        h = _rms(x_ref[rows, :], hg_ref[...]).astype(BF16) if norm_in else hg_ref[rows, :]
        acc = None
        for c0, cw in chunks:
            g = _dot(h, wg_ref[:, c0:c0 + cw])
            u = _dot(h, wu_ref[:, c0:c0 + cw])
            a = (g * _sigmoid(g) * u).astype(BF16)
            d = _dot(a, wd_ref[c0:c0 + cw, :])
            acc = d if acc is None else acc + d
        xn = x_ref[rows, :] + 0.5 * acc
        if len(rest) == 3:
            gn_ref, xo_ref, ho_ref = rest
            ho_ref[rows, :] = _rms(xn, gn_ref[...]).astype(ho_ref.dtype)
        else:
            xo_ref, = rest
        xo_ref[rows, :] = xn


def _ffn(x, h, wg, wu, wd, layer, g_in=None, g_next=None):
    assert (h is None) == (g_in is not None)
    T, D = x.shape
    Fh = wg.shape[2]
    tm = FFN_TILE
    chunks = tuple((c0, min(FFN_CHUNK, Fh - c0)) for c0 in range(0, Fh, FFN_CHUNK))
    vmem = 3 * D * Fh * 2 + 2 * tm * D * (4 + 2 + 4 + 2) + 4 * tm * D * 4
    row = pl.BlockSpec((tm, D), lambda i: (i, 0))
    vec = pl.BlockSpec((1, D), lambda i: (0, 0))
    in_specs = [row, vec if h is None else row,
                _layer_weight(wg, layer), _layer_weight(wu, layer), _layer_weight(wd, layer)]
    args = [x, g_in.reshape(1, D) if h is None else h, wg, wu, wd]
    out_specs = [row]
    out_shape = [jax.ShapeDtypeStruct((T, D), F32)]
    if g_next is not None:
        in_specs.append(vec)
        args.append(g_next.reshape(1, D))
        out_specs.append(row)
        out_shape.append(jax.ShapeDtypeStruct((T, D), BF16))
    return pl.pallas_call(
        functools.partial(_ffn_kernel, chunks=chunks, norm_in=h is None),
        grid=(T // tm,),
        in_specs=in_specs,
        out_specs=out_specs,
        out_shape=out_shape,
        compiler_params=_cparams(vmem, 1),
        name="ffn",
    )(*args)


def _retqk_body(h_ref, w_ref, cos_ref, sin_ref, o_ref, r0, n):
    rows = slice(r0, r0 + n)
    cos = cos_ref[rows, :]
    sin = sin_ref[rows, :]
    per = PROJ_CHUNK // RET_DK
    for c in range(w_ref.shape[1] // PROJ_CHUNK):
        res = _dot(h_ref[rows, :], w_ref[:, c * PROJ_CHUNK:(c + 1) * PROJ_CHUNK])
        for k in range(per):
            t = res[:, k * RET_DK:(k + 1) * RET_DK]
            r = t * cos + pltpu.roll(t, RET_DK // 2, axis=1) * sin
            hd = c * per + k
            if hd >= RET_HEADS:
                r = r * (RET_DK ** -0.5)
            o_ref[0, hd % RET_HEADS, hd // RET_HEADS, rows, :] = r.astype(o_ref.dtype)


def _retv_body(h_ref, w_ref, o_ref, r0, n):
    rows = slice(r0, r0 + n)
    per = PROJ_CHUNK // RET_DV
    for c in range(w_ref.shape[1] // PROJ_CHUNK):
        res = _dot(h_ref[rows, :], w_ref[:, c * PROJ_CHUNK:(c + 1) * PROJ_CHUNK])
        for k in range(per):
            o_ref[0, c * per + k, rows, :] = res[:, k * RET_DV:(k + 1) * RET_DV].astype(o_ref.dtype)


def _dil_body(h_ref, w_ref, qn, kn, o_ref, scr_ref, scr2_ref, d, r0, n):
    rows = slice(r0, r0 + n)
    MID = DIL_MID
    per = PROJ_CHUNK // DIL_DH
    for c in range(w_ref.shape[1] // PROJ_CHUNK):
        res = _dot(h_ref[rows, :], w_ref[:, c * PROJ_CHUNK:(c + 1) * PROJ_CHUNK])
        for k in range(per):
            t = res[:, k * DIL_DH:(k + 1) * DIL_DH]
            hd = c * per + k
            if hd < DIL_GROUP_HEADS:
                t = _rms(t, qn)
            elif hd < 2 * DIL_GROUP_HEADS:
                t = _rms(t, kn)
            if d == 1:
                o_ref[0, hd % DIL_GROUP_HEADS, hd // DIL_GROUP_HEADS, 0, rows, :] = t.astype(o_ref.dtype)
            else:
                scr_ref[hd, rows, :] = t
    if d == 1:
        return
    for hd in range(3 * DIL_GROUP_HEADS):
        out = o_ref.at[0, hd % DIL_GROUP_HEADS, hd // DIL_GROUP_HEADS]
        orow = slice(r0 // d, (r0 + n) // d)
        if d == MID:
            for r in range(d):
                out[r, orow, :] = scr_ref[hd, pl.ds(r0 + r, n // d, stride=d), :].astype(o_ref.dtype)
        else:
            assert d == MID * MID
            for r in range(MID):
                scr2_ref[r] = scr_ref[hd, pl.ds(r0 + r, n // MID, stride=MID), :]
            for r in range(MID):
                for a in range(MID):
                    out[MID * a + r, orow, :] = (
                        scr2_ref[r, pl.ds(a, n // d, stride=MID), :].astype(o_ref.dtype))


def _proj_ret_kernel(h_ref, wqk_ref, wv_ref, cos_ref, sin_ref, oqk_ref, ov_ref):
    for r0 in range(0, h_ref.shape[0], PROJ_SUB):
        _retqk_body(h_ref, wqk_ref, cos_ref, sin_ref, oqk_ref, r0, PROJ_SUB)
        _retv_body(h_ref, wv_ref, ov_ref, r0, PROJ_SUB)


def _proj_dil_kernel(h_ref, wd0_ref, wd1_ref, wd2_ref, qn_ref, kn_ref, od0_ref, od1_ref, od2_ref,
                     scr_ref, scr2_ref):
    qn = qn_ref[...] * (DIL_DH ** -0.5)
    kn = kn_ref[...]
    for i, r0 in enumerate(range(0, h_ref.shape[0], PROJ_SUB)):
        for w_ref, o_ref, (_, d) in zip((wd0_ref, wd1_ref, wd2_ref), (od0_ref, od1_ref, od2_ref), DIL_CONFIGS):
            _dil_body(h_ref, w_ref, qn, kn, o_ref, scr_ref, scr2_ref.at[i], d, r0, PROJ_SUB)


def _projections(h, B, S, wqk, wv, wdil, layer, cosf, sinf, qn, kn):
    T, D = h.shape
    tm = PROJ_TILE
    nS = S // tm
    nh_dil = 3 * DIL_GROUP_HEADS
    full = lambda a: pl.BlockSpec(a.shape, lambda i: (0,) * a.ndim)
    lw = lambda w: _layer_weight(w, layer)
    hrow = pl.BlockSpec((tm, D), lambda i: (i, 0))
    seq_idx = lambda i: (i // nS, 0, 0, i % nS, 0)
    width = lambda ws: sum(w.shape[2] for w in ws)
    n_ret = width([wqk, wv])
    qk, v = pl.pallas_call(
        _proj_ret_kernel,
        grid=(T // tm,),
        in_specs=[hrow, lw(wqk), lw(wv),
                  pl.BlockSpec((tm, RET_DK), lambda i: (i % nS, 0)),
                  pl.BlockSpec((tm, RET_DK), lambda i: (i % nS, 0))],
        out_specs=[pl.BlockSpec((1, RET_HEADS, 2, tm, RET_DK), seq_idx),
                   pl.BlockSpec((1, RET_HEADS, tm, RET_DV), lambda i: (i // nS, 0, i % nS, 0))],
        out_shape=[jax.ShapeDtypeStruct((B, RET_HEADS, 2, S, RET_DK), BF16),
                   jax.ShapeDtypeStruct((B, RET_HEADS, S, RET_DV), BF16)],
        compiler_params=_cparams(D * n_ret * 2 + 2 * tm * D * 2 + 2 * tm * n_ret * 2
                                 + 8 * PROJ_SUB * PROJ_CHUNK * 4 + 4 * tm * RET_DK * 4, 1),
        name="proj_ret",
    )(h, wqk, wv, cosf, sinf)
    n_dil = width(wdil)
    groups = pl.pallas_call(
        _proj_dil_kernel,
        grid=(T // tm,),
        in_specs=[hrow] + [lw(w) for w in wdil] + [full(qn), full(kn)],
        out_specs=[pl.BlockSpec((1, DIL_GROUP_HEADS, 3, d, tm // d, DIL_DH),
                                lambda i: (i // nS, 0, 0, 0, i % nS, 0)) for _, d in DIL_CONFIGS],
        out_shape=[jax.ShapeDtypeStruct((B, DIL_GROUP_HEADS, 3, d, S // d, DIL_DH), BF16)
                   for _, d in DIL_CONFIGS],
        scratch_shapes=[pltpu.VMEM((nh_dil, tm, DIL_DH), F32),
                        pltpu.VMEM((tm // PROJ_SUB, DIL_MID, PROJ_SUB // DIL_MID, DIL_DH), F32)],
        compiler_params=_cparams(D * n_dil * 2 + 2 * tm * D * 2 + 2 * tm * n_dil * 2
                                 + (nh_dil + 1) * tm * DIL_DH * 4 + 8 * PROJ_SUB * PROJ_CHUNK * 4, 1),
        name="proj_dil",
    )(h, *wdil, qn, kn)
    return qk, v, groups


def _ret_kernel(cdec_ref, qk_ref, vh_ref, dec_ref, ng_ref, o_ref,
                kvf_scr, kvb_scr, fst_scr, bst_scr, *, nchunks):
    C = RET_CHUNK
    q_ref, k_ref = qk_ref.at[0, 0, 0], qk_ref.at[0, 0, 1]
    v_ref = vh_ref.at[0, 0]
    hd = pl.program_id(1)
    cf = cdec_ref[hd, 0]
    cb = cdec_ref[hd, 1]

    def rows(n):
        return pl.ds(pl.multiple_of(n * C, C), C)

    def kv_body(n, carry):
        kn = k_ref[rows(n), :].astype(F32)
        vn = v_ref[rows(n), :]
        kvf_scr[n] = _dot_tn((kn * dec_ref[0, 3]).astype(BF16), vn)
        kvb_scr[n] = _dot_tn((kn * dec_ref[0, 4]).astype(BF16), vn)
        return carry

    lax.fori_loop(0, nchunks, kv_body, 0, unroll=32)

    st_f = jnp.zeros((RET_DK, RET_DV), F32)
    st_b = jnp.zeros((RET_DK, RET_DV), F32)
    for n in range(nchunks):
        fst_scr[n] = st_f.astype(BF16)
        bst_scr[nchunks - 1 - n] = st_b.astype(BF16)
        if n + 1 < nchunks:
            st_f = cf * st_f + kvf_scr[n]
            st_b = cb * st_b + kvb_scr[nchunks - 1 - n]

    G = RET_BATCH

    def out_body(t, carry):
        rw = pl.ds(pl.multiple_of(t * (G * C), G * C), G * C)
        cs = pl.ds(pl.multiple_of(t * G, G), G)
        q = q_ref[rw, :].reshape(G, C, RET_DK)
        k = k_ref[rw, :].reshape(G, C, RET_DK)
        v = v_ref[rw, :].reshape(G, C, RET_DV)
        q32 = q.astype(F32)
        s = jnp.einsum('gcd,gjd->gcj', q, k, preferred_element_type=F32) * dec_ref[0, 0]
        lhs = jnp.concatenate([s.astype(BF16), (q32 * dec_ref[0, 1]).astype(BF16),
                               (q32 * dec_ref[0, 2]).astype(BF16)], axis=2)
        rhs = jnp.concatenate([v, fst_scr[cs], bst_scr[cs]], axis=1)
        o = jnp.einsum('gck,gke->gce', lhs, rhs, preferred_element_type=F32)
        r = lax.rsqrt(jnp.mean(o * o, axis=-1, keepdims=True) + EPS)
        o_ref[0, rw, :] = (o * r * ng_ref[0]).reshape(G * C, RET_DV).astype(o_ref.dtype)
        return carry

    lax.fori_loop(0, nchunks // G, out_body, 0, unroll=4)


def _retention(qk, v, cdec, dec, ng):
    B, H, _, S, _ = qk.shape
    nchunks = S // RET_CHUNK
    vmem = (2 * (2 * S * RET_DK * 2 + S * RET_DV * 2 + S * RET_DV * 2)
            + nchunks * RET_DK * RET_DV * (4 + 4 + 2 + 2) + VMEM_TEMPS)
    return pl.pallas_call(
        functools.partial(_ret_kernel, nchunks=nchunks),
        grid=(B, H),
        in_specs=[pl.BlockSpec(memory_space=pltpu.SMEM),
                  pl.BlockSpec((1, 1, 2, S, RET_DK), lambda b, h: (b, h, 0, 0, 0)),
                  pl.BlockSpec((1, 1, S, RET_DV), lambda b, h: (b, h, 0, 0)),
                  pl.BlockSpec((1, 5, RET_CHUNK, RET_CHUNK), lambda b, h: (h, 0, 0, 0)),
                  pl.BlockSpec((1, 1, RET_DV), lambda b, h: (h, 0, 0))],
        out_specs=pl.BlockSpec((1, S, RET_DV), lambda b, h: (b, 0, h)),
        out_shape=jax.ShapeDtypeStruct((B, S, H * RET_DV), BF16),
        scratch_shapes=[pltpu.VMEM((nchunks, RET_DK, RET_DV), F32),
                        pltpu.VMEM((nchunks, RET_DK, RET_DV), F32),
                        pltpu.VMEM((nchunks, RET_DK, RET_DV), BF16),
                        pltpu.VMEM((nchunks, RET_DK, RET_DV), BF16)],
        compiler_params=_cparams(vmem, 2),
        name="retention",
    )(cdec, qk, v, dec, ng)


def _dil_kernel(g0_ref, g1_ref, g2_ref, bias_ref, o_ref, osc, lsc, msc, *, seq):
    QB, KW, DH, MID, G = DIL_QBLK, DIL_KWIN, DIL_DH, DIL_MID, DIL_BATCH
    SM = seq // MID
    (q0, k0, v0), (q1, k1, v1), (q2, k2, v2) = (
        tuple(g.at[0, 0, part] for part in range(3)) for g in (g0_ref, g1_ref, g2_ref))
    dils = tuple(d for _, d in DIL_CONFIGS)
    ones = jnp.ones((KW, DH), BF16)

    def window(n, L):
        nblk = L // QB
        if isinstance(n, int):
            i0 = n * QB
            return i0, min(max(i0 - DIL_R, 0), L - KW), (0 if n == 0 else 2 if n == nblk - 1 else 1)
        i0 = pl.multiple_of(n * QB, QB)
        w0 = pl.multiple_of(jnp.clip(i0 - DIL_R, 0, L - KW), DIL_R)
        return i0, w0, jnp.where(n == 0, 0, jnp.where(n == nblk - 1, 2, 1))

    def attend(g, tiles):
        q = jnp.stack([t[0] for t in tiles])
        k = jnp.stack([t[1] for t in tiles])
        v = jnp.stack([jnp.concatenate([t[2], ones], axis=1) for t in tiles])
        s = jnp.einsum('gqd,gkd->gqk', q, k, preferred_element_type=F32)
        s = s + jnp.stack([bias_ref[0, g, t[3]] for t in tiles])
        m = jnp.max(s, axis=-1, keepdims=True)
        p = jnp.exp(s - m).astype(BF16)
        ol = jnp.einsum('gqk,gke->gqe', p, v, preferred_element_type=F32)
        for i, t in enumerate(tiles):
            osc[g, t[4], :] = ol[i, :, :DH]
            lsc[g, t[4], :] = ol[i, :, DH:]
            msc[g, t[4], :] = jnp.broadcast_to(m[i], (QB, DH))

    def g0_body(t, carry):
        tiles = []
        for i in range(G):
            i0, w0, var = window(t * G + i, seq)
            tiles.append((q0[0, pl.ds(i0, QB), :], k0[0, pl.ds(w0, KW), :],
                          v0[0, pl.ds(w0, KW), :], var, pl.ds(i0, QB)))
        attend(0, tiles)
        return carry

    lax.fori_loop(0, seq // (QB * G), g0_body, 0, unroll=4)

    per1 = SM // (QB * G)

    def g1_body(t, carry):
        r = t // per1
        tiles = []
        for i in range(G):
            i0, w0, var = window(i if per1 == 1 else (t % per1) * G + i, SM)
            tiles.append((q1[r, pl.ds(i0, QB), :], k1[r, pl.ds(w0, KW), :],
                          v1[r, pl.ds(w0, KW), :], var,
                          pl.ds(pl.multiple_of(r * SM + i0, QB), QB)))
        attend(1, tiles)
        return carry

    lax.fori_loop(0, MID * per1, g1_body, 0, unroll=4)

    L2 = seq // dils[2]
    pend = []
    for a in range(dils[2] // MID):
        for r in range(MID):
            res = MID * a + r
            for n in range(L2 // QB):
                i0, w0, var = window(n, L2)
                pend.append((q2[res, i0:i0 + QB, :], k2[res, w0:w0 + KW, :],
                             v2[res, w0:w0 + KW, :], var,
                             pl.ds(r * SM + MID * i0 + a, QB, stride=MID)))
                if len(pend) == G:
                    attend(2, pend)
                    pend = []
    assert not pend

    CB = DIL_COMB_ROWS
    for r in range(MID):
        def comb(t, carry, r=r):
            i4 = pl.multiple_of(t * CB, CB)
            rows = (pl.ds(r + MID * i4, CB, stride=MID), pl.ds(r * SM + i4, CB), pl.ds(r * SM + i4, CB))
            ms = [msc[g, rows[g], :] for g in range(3)]
            mx = jnp.maximum(jnp.maximum(ms[0], ms[1]), ms[2])
            es = [jnp.exp(m - mx) for m in ms]
            num = sum(es[g] * osc[g, rows[g], :] for g in range(3))
            den = sum(es[g] * lsc[g, rows[g], :] for g in range(3))
            o_ref[0, 0, pl.ds(i4, CB), r * DH:(r + 1) * DH] = (num / den).astype(o_ref.dtype)
            return carry

        lax.fori_loop(0, SM // CB, comb, 0, unroll=2)


def _dilated_attention(groups, bias):
    B = groups[0].shape[0]
    S = groups[0].shape[3] * groups[0].shape[4]
    H, DH, MID = DIL_GROUP_HEADS, DIL_DH, DIL_MID
    assert [d for _, d in DIL_CONFIGS] == [1, MID, MID * MID] and S % (MID * MID * DIL_KWIN) == 0
    in_specs = [pl.BlockSpec((1, 1) + a.shape[2:], lambda b, j: (b, j, 0, 0, 0, 0)) for a in groups]
    in_specs.append(pl.BlockSpec((1, 3, 3, DIL_QBLK, DIL_KWIN), lambda b, j: (j, 0, 0, 0, 0)))
    args = list(groups) + [bias]
    vmem = (2 * 9 * S * DH * 2 + 2 * 9 * DIL_QBLK * DIL_KWIN * 4 + 2 * S * DH * 2
            + 3 * 3 * S * DH * 4 + VMEM_TEMPS)
    return pl.pallas_call(
        functools.partial(_dil_kernel, seq=S),
        grid=(B, H),
        in_specs=in_specs,
        out_specs=pl.BlockSpec((1, 1, S // MID, MID * DH), lambda b, j: (b, j, 0, 0)),
        out_shape=jax.ShapeDtypeStruct((B, H, S // MID, MID * DH), BF16),
        scratch_shapes=[pltpu.VMEM((3, S, DH), F32)] * 3,
        compiler_params=_cparams(vmem, 2),
        name="dilated_attention",
    )(*args)


def _merge_kernel(x_ref, h_ref, yr_ref, yd_ref, wrg_ref, wgt_ref, bg_ref, wro_ref, wdo_ref, wout_ref, gn_ref,
                  xo_ref, ho_ref, yd_scr):
    D = x_ref.shape[1]
    H, MID, DH = yd_scr.shape[0], DIL_MID, DIL_DH
    for r0 in range(0, x_ref.shape[0], MERGE_SUB):
        rows = slice(r0, r0 + MERGE_SUB)
        g = _dot(h_ref[rows, :], wrg_ref[...])
        a = _dot((g * _sigmoid(g) * yr_ref[rows, :].astype(F32)).astype(BF16), wro_ref[...])
        for j in range(H):
            for r in range(MID):
                yd_scr[j, pl.ds(r0 + r, MERGE_SUB // MID, stride=MID), :] = (
                    yd_ref[0, j, r0 // MID:(r0 + MERGE_SUB) // MID, r * DH:(r + 1) * DH].astype(F32))
        yd = jnp.concatenate([yd_scr[j, rows, :].astype(BF16) for j in range(H)], axis=1)
        b = _dot(yd, wdo_ref[...])
        mg = (_sigmoid(_dot(h_ref[rows, :], wgt_ref[:, :D]) + bg_ref[:, :D]) * a
              + _sigmoid(_dot(h_ref[rows, :], wgt_ref[:, D:]) + bg_ref[:, D:]) * b)
        xn = x_ref[rows, :] + _dot(mg.astype(BF16), wout_ref[...])
        xo_ref[rows, :] = xn
        ho_ref[rows, :] = _rms(xn, gn_ref[...]).astype(ho_ref.dtype)


def _merge(x, h, yr, yd, wrg, wgt, bg, wro, wdo, wout, layer, g_next):
    T, D = x.shape
    tm = MERGE_TILE
    _, H, SM, W = yd.shape
    MID = DIL_MID
    nS = SM * MID // tm
    row = lambda n: pl.BlockSpec((tm, n), lambda i: (i, 0))
    vmem = ((wrg[0].size + wgt[0].size + wro[0].size + wdo[0].size + wout[0].size) * 2
            + 2 * tm * (D * 4 + D * 2 + yr.shape[1] * 2 + H * DIL_DH * 2 + D * 4 + D * 2)
            + H * tm * DIL_DH * 4 + 6 * tm * D * 4)
    return pl.pallas_call(
        _merge_kernel,
        grid=(T // tm,),
        in_specs=[row(D), row(D), row(yr.shape[1]),
                  pl.BlockSpec((1, H, tm // MID, W), lambda i: (i // nS, 0, i % nS, 0)),
                  _layer_weight(wrg, layer), _layer_weight(wgt, layer),
                  pl.BlockSpec((1, bg.shape[1]), lambda i: (0, 0)),
                  _layer_weight(wro, layer), _layer_weight(wdo, layer), _layer_weight(wout, layer),
                  pl.BlockSpec((1, D), lambda i: (0, 0))],
        out_specs=[row(D), row(D)],
        out_shape=[jax.ShapeDtypeStruct((T, D), F32), jax.ShapeDtypeStruct((T, D), BF16)],
        scratch_shapes=[pltpu.VMEM((H, tm, DIL_DH), F32)],
        compiler_params=_cparams(vmem, 1),
        name="merge",
    )(x, h, yr, yd, wrg, wgt, bg, wro, wdo, wout, g_next.reshape(1, D))


def _t5_bucket(rel):
    nb = N_BUCKETS // 2
    max_exact = nb // 2
    ret = jnp.where(rel > 0, nb, 0)
    n = jnp.abs(rel)
    nf = jnp.maximum(n, 1).astype(F32)
    large = max_exact + (jnp.log(nf / max_exact) / math.log(MAX_DISTANCE / max_exact)
                         * (nb - max_exact)).astype(jnp.int32)
    large = jnp.minimum(large, nb - 1)
    return ret + jnp.where(n < max_exact, n, large)


def _dil_bias_tiles(rel_bias):
    Q, K, R = DIL_QBLK, DIL_KWIN, DIL_R
    P = Q + K
    per_group = []
    for gi, (_, d) in enumerate(DIL_CONFIGS):
        rb = rel_bias[:, gi * DIL_GROUP_HEADS:(gi + 1) * DIL_GROUP_HEADS].astype(F32)
        tiles = []
        for delta in (0, R, 2 * R):
            off = jnp.arange(P) - (Q - 1) - delta
            bucket = _t5_bucket(jnp.clip(off, -R, R) * d)
            hit = bucket[:, None] == jnp.arange(N_BUCKETS)
            w = jnp.sum(jnp.where(hit[..., None], rb[None], 0.0), axis=1)
            w = jnp.where((jnp.abs(off) <= R)[:, None], w, NEG)
            rows = jnp.tile(w, (Q + 1, 1))[:Q * (P + 1)].reshape(Q, P + 1, -1)
            tiles.append(jnp.flip(rows[:, :K], axis=0))
        per_group.append(jnp.stack(tiles, axis=0))
    t = jnp.stack(per_group, axis=0)
    return jnp.transpose(t, (4, 0, 1, 2, 3))


def _ret_decay_tables(decay_exp):
    C = RET_CHUNK
    lg = jnp.log1p(-jnp.exp2(-decay_exp.astype(F32)))
    lf = lg[0][:, None, None]
    lb = lg[1][:, None, None]
    idx = jnp.arange(C, dtype=F32)
    diff = idx[:, None] - idx[None, :]
    dm = jnp.where(diff[None] >= 0, jnp.exp(lf * jnp.maximum(diff, 0.0)[None]),
                   jnp.exp(lb * jnp.maximum(-diff, 0.0)[None]))
    col = lambda v: jnp.broadcast_to(v[:, :, None], (v.shape[0], C, C))
    qdf = col(jnp.exp(lg[0][:, None] * (idx + 1.0)[None]))
    qdb = col(jnp.exp(lg[1][:, None] * (C - idx)[None]))
    kdf = col(jnp.exp(lg[0][:, None] * (C - 1.0 - idx)[None]))
    kdb = col(jnp.exp(lg[1][:, None] * idx[None]))
    dec = jnp.stack([dm, qdf, qdb, kdf, kdb], axis=1)
    cdec = jnp.stack([jnp.exp(lg[0] * C), jnp.exp(lg[1] * C)], axis=1)
    return dec, cdec


def _rope_tables(S):
    half = RET_DK // 2
    pos = jnp.arange(S, dtype=F32)
    inv = ROPE_BASE ** (-jnp.arange(half, dtype=F32) / half)
    ang = pos[:, None] * inv[None, :]
    cos, sin = jnp.cos(ang), jnp.sin(ang)
    return jnp.concatenate([cos, cos], axis=-1), jnp.concatenate([-sin, sin], axis=-1)


def kernel(x, rel_bias, norm_ffn1, ffn1_gate, ffn1_up, ffn1_down, norm_mix, w_in, b_gate,
           ret_decay_exp, ret_norm, w_ret_o, dil_q_norm, dil_k_norm, w_dil_o, w_out, norm_ffn2,
           ffn2_gate, ffn2_up, ffn2_down):
    B, S, D = x.shape
    depth = w_in.shape[0]
    T = B * S
    assert S % PROJ_TILE == 0 and S % MERGE_TILE == 0
    ret_qk = RET_HEADS * RET_DK
    ret_v = RET_HEADS * RET_DV
    dil_w = DIL_GROUP_HEADS * len(DIL_CONFIGS) * DIL_DH
    o_rq, o_rv = 0, 2 * ret_qk
    o_dq = 2 * ret_qk + 2 * ret_v
    o_dk, o_dv = o_dq + dil_w, o_dq + 2 * dil_w
    o_gate = o_dq + 3 * dil_w
    gw = DIL_GROUP_HEADS * DIL_DH

    cosf, sinf = _rope_tables(S)
    bias_tiles = _dil_bias_tiles(rel_bias)
    bf = lambda a: a.astype(BF16)
    ffn1 = (bf(ffn1_gate), bf(ffn1_up), bf(ffn1_down))
    ffn2 = (bf(ffn2_gate), bf(ffn2_up), bf(ffn2_down))
    wqk, wv = bf(w_in[:, :, o_rq:o_rv]), bf(w_in[:, :, o_rv:o_rv + ret_v])
    wrg, wgt = bf(w_in[:, :, o_rv + ret_v:o_dq]), bf(w_in[:, :, o_gate:])
    wdil = [bf(jnp.concatenate([w_in[:, :, o + gi * gw:o + (gi + 1) * gw] for o in (o_dq, o_dk, o_dv)], axis=2))
            for gi in range(len(DIL_CONFIGS))]
    wro, wdo, wout = bf(w_ret_o), bf(w_dil_o), bf(w_out)

    xf = x.reshape(T, D)
    h = None
    for l in range(depth):
        xf, h = _ffn(xf, h, *ffn1, l, g_in=norm_ffn1[l] if h is None else None, g_next=norm_mix[l])
        qk, v, groups = _projections(h, B, S, wqk, wv, wdil, l, cosf, sinf,
                                     dil_q_norm[l].reshape(1, DIL_DH), dil_k_norm[l].reshape(1, DIL_DH))

        dec, cdec = _ret_decay_tables(ret_decay_exp[l])
        yr = _retention(qk, v, cdec, dec, ret_norm[l].reshape(RET_HEADS, 1, RET_DV))
        yd = _dilated_attention(groups, bias_tiles)

        xf, h = _merge(xf, h, yr.reshape(T, ret_v), yd, wrg, wgt, b_gate[l].reshape(1, 2 * D),
                       wro, wdo, wout, l, norm_ffn2[l])
        if l + 1 < depth:
            xf, h = _ffn(xf, h, *ffn2, l, g_next=norm_ffn1[l + 1])
        else:
            xf, = _ffn(xf, h, *ffn2, l)
    return xf.reshape(B, S, D)
```

```python
import functools
import math

import jax
import jax.numpy as jnp
from jax import lax
from jax.experimental import pallas as pl
from jax.experimental.pallas import tpu as pltpu

F32 = jnp.float32
BF16 = jnp.bfloat16

V7X_VMEM_BYTES = 64 * 1024 * 1024
VMEM_LIMIT_CAP = V7X_VMEM_BYTES - (4 << 20)
VMEM_TEMPS = 8 << 20

EPS = 1e-6
ROPE_BASE = 10000.0
NEG = -1e30

RET_HEADS = 4
RET_DK = 128
RET_DV = 256
RET_CHUNK = 128
RET_BATCH = 8

DIL_CONFIGS = ((128, 1), (512, 4), (2048, 16))
DIL_GROUP_HEADS = 4
DIL_DH = 128
DIL_R = 64
DIL_QBLK = 2 * DIL_R
DIL_KWIN = 4 * DIL_R
DIL_BATCH = 8
DIL_MID = 4
DIL_COMB_ROWS = 256
N_BUCKETS = 32
MAX_DISTANCE = 1024

FFN_TILE = 1024
FFN_SUB = 512
PROJ_TILE = 1024
PROJ_SUB = 512
MERGE_TILE = 1024
MERGE_SUB = 512
FFN_CHUNK = 512
PROJ_CHUNK = 256


def _cparams(vmem_bytes, ngrid):
    return pltpu.CompilerParams(
        dimension_semantics=("arbitrary",) * ngrid,
        vmem_limit_bytes=int(min(vmem_bytes, VMEM_LIMIT_CAP)),
    )


def _layer_weight(w_all, layer):
    zeros = (0,) * (w_all.ndim - 1)
    return pl.BlockSpec((None,) + w_all.shape[1:], lambda *_: (layer,) + zeros,
                        pipeline_mode=pl.Buffered(1))


def _dot(a, b):
    return jnp.dot(a, b, preferred_element_type=F32)


def _dot_nt(a, b):
    return lax.dot_general(a, b, (((1,), (1,)), ((), ())), preferred_element_type=F32)


def _dot_tn(a, b):
    return lax.dot_general(a, b, (((0,), (0,)), ((), ())), preferred_element_type=F32)


def _rms(x32, g):
    ms = jnp.mean(x32 * x32, axis=-1, keepdims=True)
    return x32 * lax.rsqrt(ms + EPS) * g


def _sigmoid(x):
    return 0.5 * jnp.tanh(0.5 * x) + 0.5


def _ffn_kernel(x_ref, hg_ref, wg_ref, wu_ref, wd_ref, *rest, chunks, norm_in):
    for r0 in range(0, x_ref.shape[0], FFN_SUB):
        rows = slice(r0, r0 + FFN_SUB)
        h = _rms(x_ref[rows, :], hg_ref[...]).astype(BF16) if norm_in else hg_ref[rows, :]
        acc = None
        for c0, cw in chunks:
            g = _dot(h, wg_ref[:, c0:c0 + cw])
            u = _dot(h, wu_ref[:, c0:c0 + cw])
            a = (g * _sigmoid(g) * u).astype(BF16)
            d = _dot(a, wd_ref[c0:c0 + cw, :])
            acc = d if acc is None else acc + d
        xn = x_ref[rows, :] + 0.5 * acc
        if len(rest) == 3:
            gn_ref, xo_ref, ho_ref = rest
            ho_ref[rows, :] = _rms(xn, gn_ref[...]).astype(ho_ref.dtype)
        else:
            xo_ref, = rest
        xo_ref[rows, :] = xn


def _ffn(x, h, wg, wu, wd, layer, g_in=None, g_next=None):
    assert (h is None) == (g_in is not None)
    T, D = x.shape
    Fh = wg.shape[2]
    tm = FFN_TILE
    chunks = tuple((c0, min(FFN_CHUNK, Fh - c0)) for c0 in range(0, Fh, FFN_CHUNK))
    vmem = 3 * D * Fh * 2 + 2 * tm * D * (4 + 2 + 4 + 2) + 4 * tm * D * 4
    row = pl.BlockSpec((tm, D), lambda i: (i, 0))
    vec = pl.BlockSpec((1, D), lambda i: (0, 0))
    in_specs = [row, vec if h is None else row,
                _layer_weight(wg, layer), _layer_weight(wu, layer), _layer_weight(wd, layer)]
    args = [x, g_in.reshape(1, D) if h is None else h, wg, wu, wd]
    out_specs = [row]
    out_shape = [jax.ShapeDtypeStruct((T, D), F32)]
    if g_next is not None:
        in_specs.append(vec)
        args.append(g_next.reshape(1, D))
        out_specs.append(row)
        out_shape.append(jax.ShapeDtypeStruct((T, D), BF16))
    return pl.pallas_call(
        functools.partial(_ffn_kernel, chunks=chunks, norm_in=h is None),
        grid=(T // tm,),
        in_specs=in_specs,
        out_specs=out_specs,
        out_shape=out_shape,
        compiler_params=_cparams(vmem, 1),
        name="ffn",
    )(*args)


def _retqk_body(h_ref, w_ref, cos_ref, sin_ref, o_ref, r0, n):
    rows = slice(r0, r0 + n)
    cos = cos_ref[rows, :]
    sin = sin_ref[rows, :]
    per = PROJ_CHUNK // RET_DK
    for c in range(w_ref.shape[1] // PROJ_CHUNK):
        res = _dot(h_ref[rows, :], w_ref[:, c * PROJ_CHUNK:(c + 1) * PROJ_CHUNK])
        for k in range(per):
            t = res[:, k * RET_DK:(k + 1) * RET_DK]
            r = t * cos + pltpu.roll(t, RET_DK // 2, axis=1) * sin
            hd = c * per + k
            if hd >= RET_HEADS:
                r = r * (RET_DK ** -0.5)
            o_ref[0, hd % RET_HEADS, hd // RET_HEADS, rows, :] = r.astype(o_ref.dtype)


def _retv_body(h_ref, w_ref, o_ref, r0, n):
    rows = slice(r0, r0 + n)
    per = PROJ_CHUNK // RET_DV
    for c in range(w_ref.shape[1] // PROJ_CHUNK):
        res = _dot(h_ref[rows, :], w_ref[:, c * PROJ_CHUNK:(c + 1) * PROJ_CHUNK])
        for k in range(per):
            o_ref[0, c * per + k, rows, :] = res[:, k * RET_DV:(k + 1) * RET_DV].astype(o_ref.dtype)


def _dil_body(h_ref, w_ref, qn, kn, o_ref, scr_ref, scr2_ref, d, r0, n):
    rows = slice(r0, r0 + n)
    MID = DIL_MID
    per = PROJ_CHUNK // DIL_DH
    for c in range(w_ref.shape[1] // PROJ_CHUNK):
        res = _dot(h_ref[rows, :], w_ref[:, c * PROJ_CHUNK:(c + 1) * PROJ_CHUNK])
        for k in range(per):
            t = res[:, k * DIL_DH:(k + 1) * DIL_DH]
            hd = c * per + k
            if hd < DIL_GROUP_HEADS:
                t = _rms(t, qn)
            elif hd < 2 * DIL_GROUP_HEADS:
                t = _rms(t, kn)
            if d == 1:
                o_ref[0, hd % DIL_GROUP_HEADS, hd // DIL_GROUP_HEADS, 0, rows, :] = t.astype(o_ref.dtype)
            else:
                scr_ref[hd, rows, :] = t
    if d == 1:
        return
    for hd in range(3 * DIL_GROUP_HEADS):
        out = o_ref.at[0, hd % DIL_GROUP_HEADS, hd // DIL_GROUP_HEADS]
        orow = slice(r0 // d, (r0 + n) // d)
        if d == MID:
            for r in range(d):
                out[r, orow, :] = scr_ref[hd, pl.ds(r0 + r, n // d, stride=d), :].astype(o_ref.dtype)
        else:
            assert d == MID * MID
            for r in range(MID):
                scr2_ref[r] = scr_ref[hd, pl.ds(r0 + r, n // MID, stride=MID), :]
            for r in range(MID):
                for a in range(MID):
                    out[MID * a + r, orow, :] = (
                        scr2_ref[r, pl.ds(a, n // d, stride=MID), :].astype(o_ref.dtype))


def _proj_kernel(h_ref, wqk_ref, wv_ref, wd0_ref, wd1_ref, wd2_ref, cos_ref, sin_ref, qn_ref, kn_ref,
                 oqk_ref, ov_ref, od0_ref, od1_ref, od2_ref, scr_ref, scr2_ref):
    qn = qn_ref[...] * (DIL_DH ** -0.5)
    kn = kn_ref[...]
    for i, r0 in enumerate(range(0, h_ref.shape[0], PROJ_SUB)):
        _retqk_body(h_ref, wqk_ref, cos_ref, sin_ref, oqk_ref, r0, PROJ_SUB)
        _retv_body(h_ref, wv_ref, ov_ref, r0, PROJ_SUB)
        for w_ref, o_ref, (_, d) in zip((wd0_ref, wd1_ref, wd2_ref), (od0_ref, od1_ref, od2_ref), DIL_CONFIGS):
            _dil_body(h_ref, w_ref, qn, kn, o_ref, scr_ref, scr2_ref.at[i], d, r0, PROJ_SUB)


def _projections(h, B, S, wqk, wv, wdil, layer, cosf, sinf, qn, kn):
    T, D = h.shape
    tm = PROJ_TILE
    nS = S // tm
    nh_dil = 3 * DIL_GROUP_HEADS
    full = lambda a: pl.BlockSpec(a.shape, lambda i: (0,) * a.ndim)
    lw = lambda w: _layer_weight(w, layer)
    rope = pl.BlockSpec((tm, RET_DK), lambda i: (i % nS, 0))
    n_all = sum(w.shape[2] for w in [wqk, wv] + list(wdil))
    out = pl.pallas_call(
        _proj_kernel,
        grid=(T // tm,),
        in_specs=([pl.BlockSpec((tm, D), lambda i: (i, 0)), lw(wqk), lw(wv)] + [lw(w) for w in wdil]
                  + [rope, rope, full(qn), full(kn)]),
        out_specs=([pl.BlockSpec((1, RET_HEADS, 2, tm, RET_DK), lambda i: (i // nS, 0, 0, i % nS, 0)),
                    pl.BlockSpec((1, RET_HEADS, tm, RET_DV), lambda i: (i // nS, 0, i % nS, 0))]
                   + [pl.BlockSpec((1, DIL_GROUP_HEADS, 3, d, tm // d, DIL_DH),
                                   lambda i: (i // nS, 0, 0, 0, i % nS, 0)) for _, d in DIL_CONFIGS]),
        out_shape=([jax.ShapeDtypeStruct((B, RET_HEADS, 2, S, RET_DK), BF16),
                    jax.ShapeDtypeStruct((B, RET_HEADS, S, RET_DV), BF16)]
                   + [jax.ShapeDtypeStruct((B, DIL_GROUP_HEADS, 3, d, S // d, DIL_DH), BF16)
                      for _, d in DIL_CONFIGS]),
        scratch_shapes=[pltpu.VMEM((nh_dil, tm, DIL_DH), F32),
                        pltpu.VMEM((tm // PROJ_SUB, DIL_MID, PROJ_SUB // DIL_MID, DIL_DH), F32)],
        compiler_params=_cparams(D * n_all * 2 + 2 * tm * D * 2 + 2 * tm * n_all * 2 + 4 * tm * RET_DK * 4
                                 + (nh_dil + 1) * tm * DIL_DH * 4 + 8 * PROJ_SUB * PROJ_CHUNK * 4, 1),
        name="projections",
    )(h, wqk, wv, *wdil, cosf, sinf, qn, kn)
    return out[0], out[1], out[2:]


def _ret_kernel(cdec_ref, qk_ref, vh_ref, dec_ref, ng_ref, o_ref,
                kvf_scr, kvb_scr, fst_scr, bst_scr, *, nchunks):
    C = RET_CHUNK
    q_ref, k_ref = qk_ref.at[0, 0, 0], qk_ref.at[0, 0, 1]
    v_ref = vh_ref.at[0, 0]
    hd = pl.program_id(1)
    cf = cdec_ref[hd, 0]
    cb = cdec_ref[hd, 1]

    def rows(n):
        return pl.ds(pl.multiple_of(n * C, C), C)

    def kv_body(n, carry):
        kn = k_ref[rows(n), :].astype(F32)
        vn = v_ref[rows(n), :]
        kvf_scr[n] = _dot_tn((kn * dec_ref[0, 3]).astype(BF16), vn)
        kvb_scr[n] = _dot_tn((kn * dec_ref[0, 4]).astype(BF16), vn)
        return carry

    lax.fori_loop(0, nchunks, kv_body, 0, unroll=32)

    st_f = jnp.zeros((RET_DK, RET_DV), F32)
    st_b = jnp.zeros((RET_DK, RET_DV), F32)
    for n in range(nchunks):
        fst_scr[n] = st_f.astype(BF16)
        bst_scr[nchunks - 1 - n] = st_b.astype(BF16)
        if n + 1 < nchunks:
            st_f = cf * st_f + kvf_scr[n]
            st_b = cb * st_b + kvb_scr[nchunks - 1 - n]

    G = RET_BATCH

    def out_body(t, carry):
        rw = pl.ds(pl.multiple_of(t * (G * C), G * C), G * C)
        cs = pl.ds(pl.multiple_of(t * G, G), G)
        q = q_ref[rw, :].reshape(G, C, RET_DK)
        k = k_ref[rw, :].reshape(G, C, RET_DK)
        v = v_ref[rw, :].reshape(G, C, RET_DV)
        q32 = q.astype(F32)
        s = jnp.einsum('gcd,gjd->gcj', q, k, preferred_element_type=F32) * dec_ref[0, 0]
        lhs = jnp.concatenate([s.astype(BF16), (q32 * dec_ref[0, 1]).astype(BF16),
                               (q32 * dec_ref[0, 2]).astype(BF16)], axis=2)
        rhs = jnp.concatenate([v, fst_scr[cs], bst_scr[cs]], axis=1)
        o = jnp.einsum('gck,gke->gce', lhs, rhs, preferred_element_type=F32)
        r = lax.rsqrt(jnp.mean(o * o, axis=-1, keepdims=True) + EPS)
        o_ref[0, rw, :] = (o * r * ng_ref[0]).reshape(G * C, RET_DV).astype(o_ref.dtype)
        return carry

    lax.fori_loop(0, nchunks // G, out_body, 0, unroll=4)


def _retention(qk, v, cdec, dec, ng):
    B, H, _, S, _ = qk.shape
    nchunks = S // RET_CHUNK
    vmem = (2 * (2 * S * RET_DK * 2 + S * RET_DV * 2 + S * RET_DV * 2)
            + nchunks * RET_DK * RET_DV * (4 + 4 + 2 + 2) + VMEM_TEMPS)
    return pl.pallas_call(
        functools.partial(_ret_kernel, nchunks=nchunks),
        grid=(B, H),
        in_specs=[pl.BlockSpec(memory_space=pltpu.SMEM),
                  pl.BlockSpec((1, 1, 2, S, RET_DK), lambda b, h: (b, h, 0, 0, 0)),
                  pl.BlockSpec((1, 1, S, RET_DV), lambda b, h: (b, h, 0, 0)),
                  pl.BlockSpec((1, 5, RET_CHUNK, RET_CHUNK), lambda b, h: (h, 0, 0, 0)),
                  pl.BlockSpec((1, 1, RET_DV), lambda b, h: (h, 0, 0))],
        out_specs=pl.BlockSpec((1, S, RET_DV), lambda b, h: (b, 0, h)),
        out_shape=jax.ShapeDtypeStruct((B, S, H * RET_DV), BF16),
        scratch_shapes=[pltpu.VMEM((nchunks, RET_DK, RET_DV), F32),
                        pltpu.VMEM((nchunks, RET_DK, RET_DV), F32),
                        pltpu.VMEM((nchunks, RET_DK, RET_DV), BF16),
                        pltpu.VMEM((nchunks, RET_DK, RET_DV), BF16)],
        compiler_params=_cparams(vmem, 2),
        name="retention",
    )(cdec, qk, v, dec, ng)


def _dil_kernel(g0_ref, g1_ref, g2_ref, bias_ref, o_ref, osc, lsc, msc, *, seq):
    QB, KW, DH, MID, G = DIL_QBLK, DIL_KWIN, DIL_DH, DIL_MID, DIL_BATCH
    SM = seq // MID
    (q0, k0, v0), (q1, k1, v1), (q2, k2, v2) = (
        tuple(g.at[0, 0, part] for part in range(3)) for g in (g0_ref, g1_ref, g2_ref))
    dils = tuple(d for _, d in DIL_CONFIGS)
    ones = jnp.ones((KW, DH), BF16)

    def window(n, L):
        nblk = L // QB
        if isinstance(n, int):
            i0 = n * QB
            return i0, min(max(i0 - DIL_R, 0), L - KW), (0 if n == 0 else 2 if n == nblk - 1 else 1)
        i0 = pl.multiple_of(n * QB, QB)
        w0 = pl.multiple_of(jnp.clip(i0 - DIL_R, 0, L - KW), DIL_R)
        return i0, w0, jnp.where(n == 0, 0, jnp.where(n == nblk - 1, 2, 1))

    def attend(g, tiles):
        q = jnp.stack([t[0] for t in tiles])
        k = jnp.stack([t[1] for t in tiles])
        v = jnp.stack([jnp.concatenate([t[2], ones], axis=1) for t in tiles])
        s = jnp.einsum('gqd,gkd->gqk', q, k, preferred_element_type=F32)
        s = s + jnp.stack([bias_ref[0, g, t[3]] for t in tiles])
        m = jnp.max(s, axis=-1, keepdims=True)
        p = jnp.exp(s - m).astype(BF16)
        ol = jnp.einsum('gqk,gke->gqe', p, v, preferred_element_type=F32)
        for i, t in enumerate(tiles):
            osc[g, t[4], :] = ol[i, :, :DH]
            lsc[g, t[4], :] = ol[i, :, DH:]
            msc[g, t[4], :] = jnp.broadcast_to(m[i], (QB, DH))

    def g0_body(t, carry):
        tiles = []
        for i in range(G):
            i0, w0, var = window(t * G + i, seq)
            tiles.append((q0[0, pl.ds(i0, QB), :], k0[0, pl.ds(w0, KW), :],
                          v0[0, pl.ds(w0, KW), :], var, pl.ds(i0, QB)))
        attend(0, tiles)
        return carry

    lax.fori_loop(0, seq // (QB * G), g0_body, 0, unroll=4)

    per1 = SM // (QB * G)

    def g1_body(t, carry):
        r = t // per1
        tiles = []
        for i in range(G):
            i0, w0, var = window(i if per1 == 1 else (t % per1) * G + i, SM)
            tiles.append((q1[r, pl.ds(i0, QB), :], k1[r, pl.ds(w0, KW), :],
                          v1[r, pl.ds(w0, KW), :], var,
                          pl.ds(pl.multiple_of(r * SM + i0, QB), QB)))
        attend(1, tiles)
        return carry

    lax.fori_loop(0, MID * per1, g1_body, 0, unroll=4)

    L2 = seq // dils[2]
    pend = []
    for a in range(dils[2] // MID):
        for r in range(MID):
            res = MID * a + r
            for n in range(L2 // QB):
                i0, w0, var = window(n, L2)
                pend.append((q2[res, i0:i0 + QB, :], k2[res, w0:w0 + KW, :],
                             v2[res, w0:w0 + KW, :], var,
                             pl.ds(r * SM + MID * i0 + a, QB, stride=MID)))
                if len(pend) == G:
                    attend(2, pend)
                    pend = []
    assert not pend

    CB = DIL_COMB_ROWS
    for r in range(MID):
        def comb(t, carry, r=r):
            i4 = pl.multiple_of(t * CB, CB)
            rows = (pl.ds(r + MID * i4, CB, stride=MID), pl.ds(r * SM + i4, CB), pl.ds(r * SM + i4, CB))
            ms = [msc[g, rows[g], :] for g in range(3)]
            mx = jnp.maximum(jnp.maximum(ms[0], ms[1]), ms[2])
            es = [jnp.exp(m - mx) for m in ms]
            num = sum(es[g] * osc[g, rows[g], :] for g in range(3))
            den = sum(es[g] * lsc[g, rows[g], :] for g in range(3))
            o_ref[0, 0, pl.ds(i4, CB), r * DH:(r + 1) * DH] = (num / den).astype(o_ref.dtype)
            return carry

        lax.fori_loop(0, SM // CB, comb, 0, unroll=2)


def _dilated_attention(groups, bias):
    B = groups[0].shape[0]
    S = groups[0].shape[3] * groups[0].shape[4]
    H, DH, MID = DIL_GROUP_HEADS, DIL_DH, DIL_MID
    assert [d for _, d in DIL_CONFIGS] == [1, MID, MID * MID] and S % (MID * MID * DIL_KWIN) == 0
    in_specs = [pl.BlockSpec((1, 1) + a.shape[2:], lambda b, j: (b, j, 0, 0, 0, 0)) for a in groups]
    in_specs.append(pl.BlockSpec((1, 3, 3, DIL_QBLK, DIL_KWIN), lambda b, j: (j, 0, 0, 0, 0)))
    args = list(groups) + [bias]
    vmem = (2 * 9 * S * DH * 2 + 2 * 9 * DIL_QBLK * DIL_KWIN * 4 + 2 * S * DH * 2
            + 3 * 3 * S * DH * 4 + VMEM_TEMPS)
    return pl.pallas_call(
        functools.partial(_dil_kernel, seq=S),
        grid=(B, H),
        in_specs=in_specs,
        out_specs=pl.BlockSpec((1, 1, S // MID, MID * DH), lambda b, j: (b, j, 0, 0)),
        out_shape=jax.ShapeDtypeStruct((B, H, S // MID, MID * DH), BF16),
        scratch_shapes=[pltpu.VMEM((3, S, DH), F32)] * 3,
        compiler_params=_cparams(vmem, 2),
        name="dilated_attention",
    )(*args)


def _merge_kernel(x_ref, h_ref, yr_ref, yd_ref, wrg_ref, wgt_ref, bg_ref, wro_ref, wdo_ref, wout_ref, gn_ref,
                  xo_ref, ho_ref, yd_scr):
    D = x_ref.shape[1]
    H, MID, DH = yd_scr.shape[0], DIL_MID, DIL_DH
    for r0 in range(0, x_ref.shape[0], MERGE_SUB):
        rows = slice(r0, r0 + MERGE_SUB)
        g = _dot(h_ref[rows, :], wrg_ref[...])
        a = _dot((g * _sigmoid(g) * yr_ref[rows, :].astype(F32)).astype(BF16), wro_ref[...])
        for j in range(H):
            for r in range(MID):
                yd_scr[j, pl.ds(r0 + r, MERGE_SUB // MID, stride=MID), :] = (
                    yd_ref[0, j, r0 // MID:(r0 + MERGE_SUB) // MID, r * DH:(r + 1) * DH].astype(F32))
        yd = jnp.concatenate([yd_scr[j, rows, :].astype(BF16) for j in range(H)], axis=1)
        b = _dot(yd, wdo_ref[...])
        mg = (_sigmoid(_dot(h_ref[rows, :], wgt_ref[:, :D]) + bg_ref[:, :D]) * a
              + _sigmoid(_dot(h_ref[rows, :], wgt_ref[:, D:]) + bg_ref[:, D:]) * b)
        xn = x_ref[rows, :] + _dot(mg.astype(BF16), wout_ref[...])
        xo_ref[rows, :] = xn
        ho_ref[rows, :] = _rms(xn, gn_ref[...]).astype(ho_ref.dtype)


def _merge(x, h, yr, yd, wrg, wgt, bg, wro, wdo, wout, layer, g_next):
    T, D = x.shape
    tm = MERGE_TILE
    _, H, SM, W = yd.shape
    MID = DIL_MID
    nS = SM * MID // tm
    row = lambda n: pl.BlockSpec((tm, n), lambda i: (i, 0))
    vmem = ((wrg[0].size + wgt[0].size + wro[0].size + wdo[0].size + wout[0].size) * 2
            + 2 * tm * (D * 4 + D * 2 + yr.shape[1] * 2 + H * DIL_DH * 2 + D * 4 + D * 2)
            + H * tm * DIL_DH * 4 + 6 * tm * D * 4)
    return pl.pallas_call(
        _merge_kernel,
        grid=(T // tm,),
        in_specs=[row(D), row(D), row(yr.shape[1]),
                  pl.BlockSpec((1, H, tm // MID, W), lambda i: (i // nS, 0, i % nS, 0)),
                  _layer_weight(wrg, layer), _layer_weight(wgt, layer),
                  pl.BlockSpec((1, bg.shape[1]), lambda i: (0, 0)),
                  _layer_weight(wro, layer), _layer_weight(wdo, layer), _layer_weight(wout, layer),
                  pl.BlockSpec((1, D), lambda i: (0, 0))],
        out_specs=[row(D), row(D)],
        out_shape=[jax.ShapeDtypeStruct((T, D), F32), jax.ShapeDtypeStruct((T, D), BF16)],
        scratch_shapes=[pltpu.VMEM((H, tm, DIL_DH), F32)],
        compiler_params=_cparams(vmem, 1),
        name="merge",
    )(x, h, yr, yd, wrg, wgt, bg, wro, wdo, wout, g_next.reshape(1, D))


def _t5_bucket(rel):
    nb = N_BUCKETS // 2
    max_exact = nb // 2
    ret = jnp.where(rel > 0, nb, 0)
    n = jnp.abs(rel)
    nf = jnp.maximum(n, 1).astype(F32)
    large = max_exact + (jnp.log(nf / max_exact) / math.log(MAX_DISTANCE / max_exact)
                         * (nb - max_exact)).astype(jnp.int32)
    large = jnp.minimum(large, nb - 1)
    return ret + jnp.where(n < max_exact, n, large)


def _dil_bias_tiles(rel_bias):
    Q, K, R = DIL_QBLK, DIL_KWIN, DIL_R
    P = Q + K
    per_group = []
    for gi, (_, d) in enumerate(DIL_CONFIGS):
        rb = rel_bias[:, gi * DIL_GROUP_HEADS:(gi + 1) * DIL_GROUP_HEADS].astype(F32)
        tiles = []
        for delta in (0, R, 2 * R):
            off = jnp.arange(P) - (Q - 1) - delta
            bucket = _t5_bucket(jnp.clip(off, -R, R) * d)
            hit = bucket[:, None] == jnp.arange(N_BUCKETS)
            w = jnp.sum(jnp.where(hit[..., None], rb[None], 0.0), axis=1)
            w = jnp.where((jnp.abs(off) <= R)[:, None], w, NEG)
            rows = jnp.tile(w, (Q + 1, 1))[:Q * (P + 1)].reshape(Q, P + 1, -1)
            tiles.append(jnp.flip(rows[:, :K], axis=0))
        per_group.append(jnp.stack(tiles, axis=0))
    t = jnp.stack(per_group, axis=0)
    return jnp.transpose(t, (4, 0, 1, 2, 3))


def _ret_decay_tables(decay_exp):
    C = RET_CHUNK
    lg = jnp.log1p(-jnp.exp2(-decay_exp.astype(F32)))
    lf = lg[0][:, None, None]
    lb = lg[1][:, None, None]
    idx = jnp.arange(C, dtype=F32)
    diff = idx[:, None] - idx[None, :]
    dm = jnp.where(diff[None] >= 0, jnp.exp(lf * jnp.maximum(diff, 0.0)[None]),
                   jnp.exp(lb * jnp.maximum(-diff, 0.0)[None]))
    col = lambda v: jnp.broadcast_to(v[:, :, None], (v.shape[0], C, C))
    qdf = col(jnp.exp(lg[0][:, None] * (idx + 1.0)[None]))
    qdb = col(jnp.exp(lg[1][:, None] * (C - idx)[None]))
    kdf = col(jnp.exp(lg[0][:, None] * (C - 1.0 - idx)[None]))
    kdb = col(jnp.exp(lg[1][:, None] * idx[None]))
    dec = jnp.stack([dm, qdf, qdb, kdf, kdb], axis=1)
    cdec = jnp.stack([jnp.exp(lg[0] * C), jnp.exp(lg[1] * C)], axis=1)
    return dec, cdec


def _rope_tables(S):
    half = RET_DK // 2
    pos = jnp.arange(S, dtype=F32)
    inv = ROPE_BASE ** (-jnp.arange(half, dtype=F32) / half)
    ang = pos[:, None] * inv[None, :]
    cos, sin = jnp.cos(ang), jnp.sin(ang)
    return jnp.concatenate([cos, cos], axis=-1), jnp.concatenate([-sin, sin], axis=-1)


def kernel(x, rel_bias, norm_ffn1, ffn1_gate, ffn1_up, ffn1_down, norm_mix, w_in, b_gate,
           ret_decay_exp, ret_norm, w_ret_o, dil_q_norm, dil_k_norm, w_dil_o, w_out, norm_ffn2,
           ffn2_gate, ffn2_up, ffn2_down):
    B, S, D = x.shape
    depth = w_in.shape[0]
    T = B * S
    assert S % PROJ_TILE == 0 and S % MERGE_TILE == 0
    ret_qk = RET_HEADS * RET_DK
    ret_v = RET_HEADS * RET_DV
    dil_w = DIL_GROUP_HEADS * len(DIL_CONFIGS) * DIL_DH
    o_rq, o_rv = 0, 2 * ret_qk
    o_dq = 2 * ret_qk + 2 * ret_v
    o_dk, o_dv = o_dq + dil_w, o_dq + 2 * dil_w
    o_gate = o_dq + 3 * dil_w
    gw = DIL_GROUP_HEADS * DIL_DH

    cosf, sinf = _rope_tables(S)
    bias_tiles = _dil_bias_tiles(rel_bias)
    bf = lambda a: a.astype(BF16)
    ffn1 = (bf(ffn1_gate), bf(ffn1_up), bf(ffn1_down))
    ffn2 = (bf(ffn2_gate), bf(ffn2_up), bf(ffn2_down))
    wqk, wv = bf(w_in[:, :, o_rq:o_rv]), bf(w_in[:, :, o_rv:o_rv + ret_v])
    wrg, wgt = bf(w_in[:, :, o_rv + ret_v:o_dq]), bf(w_in[:, :, o_gate:])
    wdil = [bf(jnp.concatenate([w_in[:, :, o + gi * gw:o + (gi + 1) * gw] for o in (o_dq, o_dk, o_dv)], axis=2))
            for gi in range(len(DIL_CONFIGS))]
    wro, wdo, wout = bf(w_ret_o), bf(w_dil_o), bf(w_out)

    xf = x.reshape(T, D)
    h = None
    for l in range(depth):
        xf, h = _ffn(xf, h, *ffn1, l, g_in=norm_ffn1[l] if h is None else None, g_next=norm_mix[l])
        qk, v, groups = _projections(h, B, S, wqk, wv, wdil, l, cosf, sinf,
                                     dil_q_norm[l].reshape(1, DIL_DH), dil_k_norm[l].reshape(1, DIL_DH))

        dec, cdec = _ret_decay_tables(ret_decay_exp[l])
        yr = _retention(qk, v, cdec, dec, ret_norm[l].reshape(RET_HEADS, 1, RET_DV))
        yd = _dilated_attention(groups, bias_tiles)

        xf, h = _merge(xf, h, yr.reshape(T, ret_v), yd, wrg, wgt, b_gate[l].reshape(1, 2 * D),
                       wro, wdo, wout, l, norm_ffn2[l])
        if l + 1 < depth:
            xf, h = _ffn(xf, h, *ffn2, l, g_next=norm_ffn1[l + 1])
        else:
            xf, = _ffn(xf, h, *ffn2, l)
    return xf.reshape(B, S, D)
```
